```python
import math
import jax, jax.numpy as jnp
from jax import lax
import numpy as np

D_MODEL = 1024
BATCH = 4
SEQ = 4096
DEPTH = 1
DEC_BATCH = 128
DEC_SEQ = 1
PAST_LEN = 8192
PAGE_SIZE = 128

HEAD_DIM = 64
N_ATTN_HEADS = 16
D_ATTN = N_ATTN_HEADS * HEAD_DIM
DILATED_CONFIGS = ((128, 1), (512, 4), (2048, 16))
MAX_WINDOW = 2048
ATTN_BLOCK = 128
ROPE_THETA = 10000.0
NEG_INF = -1e30
N_SSM_HEADS = 16
SSM_HEAD_DIM = 64
D_SSM = N_SSM_HEADS * SSM_HEAD_DIM
SSM_GROUPS = 4
D_STATE = 128
CONV_WIDTH = 4
D_CONV = D_SSM + 2 * SSM_GROUPS * D_STATE
SSD_CHUNK = 128
D_MIX = D_ATTN + D_SSM
D_IN = 3 * D_ATTN + D_SSM + D_CONV + N_SSM_HEADS
SPLIT_POINTS = [D_ATTN, 2 * D_ATTN, 3 * D_ATTN, 3 * D_ATTN + D_SSM, 3 * D_ATTN + D_SSM + D_CONV]
D_FF = 4 * D_MODEL
N_MOD = 6
EPS = 1e-6

kernel_name = 'hybrid_dilated_attn_ssd_step'


def rms_norm(x):
    xf = x.astype(jnp.float32)
    return (xf * lax.rsqrt(jnp.mean(xf * xf, axis=-1, keepdims=True) + EPS)).astype(x.dtype)


def modulate(x, shift, scale):
    return rms_norm(x) * (1.0 + scale) + shift


def rope(t, pos):
    half = HEAD_DIM // 2
    inv_freq = ROPE_THETA ** (-jnp.arange(half, dtype=jnp.float32) / half)
    ang = pos.astype(jnp.float32)[:, None] * inv_freq[None, :]
    cos = jnp.cos(ang)[None, :, None, :]
    sin = jnp.sin(ang)[None, :, None, :]
    tf = t.astype(jnp.float32)
    t1, t2 = tf[..., :half], tf[..., half:]
    return jnp.concatenate([t1 * cos - t2 * sin, t1 * sin + t2 * cos], axis=-1).astype(t.dtype)


def dilated_band_attention(q, k, v, dil, n_steps):
    b, S, H, Dh = q.shape
    span = dil * ATTN_BLOCK
    s_pad = -(-S // span) * span
    L = s_pad // dil
    nb = L // ATTN_BLOCK

    def to_blocks(t):
        t = jnp.pad(t, ((0, 0), (0, s_pad - S), (0, 0), (0, 0)))
        t = t.reshape(b, L, dil, H, Dh).transpose(0, 2, 1, 3, 4)
        return t.reshape(b, dil, nb, ATTN_BLOCK, H, Dh)

    def with_prev(t):
        prev = jnp.pad(t, ((0, 0), (0, 0), (1, 0), (0, 0), (0, 0), (0, 0)))[:, :, :-1]
        return jnp.concatenate([prev, t], axis=3)

    qb = to_blocks(q)
    kc = with_prev(to_blocks(k))
    vc = with_prev(to_blocks(v))
    s = jnp.einsum('brnqhd,brnkhd->brnhqk', qb, kc).astype(jnp.float32)
    qi = jnp.arange(ATTN_BLOCK)[:, None] + ATTN_BLOCK
    kj = jnp.arange(2 * ATTN_BLOCK)[None, :]
    dist = qi - kj
    blk = jnp.arange(nb)[:, None, None]
    valid = (dist >= 0) & (dist <= n_steps) & ((blk > 0) | (kj >= ATTN_BLOCK))
    s = jnp.where(valid[None, None, :, None], s, NEG_INF)
    m = jnp.max(s, axis=-1, keepdims=True)
    p = jnp.exp(s - m)
    l = jnp.sum(p, axis=-1, keepdims=True)
    o = jnp.einsum('brnhqk,brnkhd->brnqhd', p / l, vc.astype(jnp.float32))
    lse = (m + jnp.log(l))[..., 0]
    o = o.reshape(b, dil, L, H, Dh).transpose(0, 2, 1, 3, 4).reshape(b, s_pad, H, Dh)[:, :S]
    lse = lse.transpose(0, 1, 2, 4, 3).reshape(b, dil, L, H).transpose(0, 2, 1, 3).reshape(b, s_pad, H)[:, :S]
    return o, lse


def dilated_cached_attention(q, k_all, v_all, dil, n_steps):
    T = q.shape[1]
    offset = k_all.shape[1] - T
    idx = (offset + jnp.arange(T))[:, None] - dil * jnp.arange(n_steps + 1)[None, :]
    valid = idx >= 0
    idx = jnp.clip(idx, 0)
    kg = jnp.take(k_all, idx, axis=1)
    vg = jnp.take(v_all, idx, axis=1)
    s = jnp.einsum('bthd,btjhd->bthj', q, kg).astype(jnp.float32)
    s = jnp.where(valid[None, :, None, :], s, NEG_INF)
    m = jnp.max(s, axis=-1, keepdims=True)
    p = jnp.exp(s - m)
    l = jnp.sum(p, axis=-1, keepdims=True)
    o = jnp.einsum('bthj,btjhd->bthd', p / l, vg.astype(jnp.float32))
    return o, (m + jnp.log(l))[..., 0]


def causal_dwconv(x_ext, conv_w, conv_b):
    y = lax.conv_general_dilated(x_ext, conv_w[:, None, :].astype(x_ext.dtype), window_strides=(1,), padding='VALID',
                                 dimension_numbers=('NWC', 'WIO', 'NWC'), feature_group_count=x_ext.shape[-1])
    return jax.nn.silu(y + conv_b)


def ssd_chunked(xs, dt, A, Bm, Cm, h0):
    b, S, G, Hg, P = xs.shape
    N = Bm.shape[-1]
    nc = S // SSD_CHUNK
    xc = xs.reshape(b, nc, SSD_CHUNK, G, Hg, P)
    dtc = dt.reshape(b, nc, SSD_CHUNK, G, Hg)
    Bc = Bm.reshape(b, nc, SSD_CHUNK, G, N)
    Cc = Cm.reshape(b, nc, SSD_CHUNK, G, N)
    cum = jnp.cumsum(dtc * A, axis=2)
    seg = cum[:, :, :, None] - cum[:, :, None, :]
    tril = jnp.tril(jnp.ones((SSD_CHUNK, SSD_CHUNK), bool))[:, :, None, None]
    decay = jnp.exp(jnp.where(tril, seg, -jnp.inf))
    y_intra = jnp.einsum('bclgn,bcsgn,bclsgh,bcsgh,bcsghp->bclghp', Cc, Bc, decay, dtc, xc)
    decay_to_end = jnp.exp(cum[:, :, -1:] - cum)
    chunk_state = jnp.einsum('bclgh,bclgh,bclghp,bclgn->bcghpn', decay_to_end, dtc, xc, Bc)
    chunk_decay = jnp.exp(cum[:, :, -1])

    def step(h, inp):
        cs, cd = inp
        return h * cd[..., None, None] + cs, h

    h_last, h_starts = lax.scan(step, h0, (jnp.moveaxis(chunk_state, 1, 0), jnp.moveaxis(chunk_decay, 1, 0)))
    h_starts = jnp.moveaxis(h_starts, 0, 1)
    y_inter = jnp.einsum('bclgn,bcghpn,bclgh->bclghp', Cc, h_starts, jnp.exp(cum))
    return (y_intra + y_inter).reshape(b, S, G, Hg, P), h_last


def ssd_recurrent(xs, dt, A, Bm, Cm, h0):
    def step(h, inp):
        x_t, dt_t, b_t, c_t = inp
        h = h * jnp.exp(dt_t * A)[..., None, None] + jnp.einsum('bgh,bghp,bgn->bghpn', dt_t, x_t, b_t)
        return h, jnp.einsum('bgn,bghpn->bghp', c_t, h)

    h_last, ys = lax.scan(step, h0, (jnp.moveaxis(xs, 1, 0), jnp.moveaxis(dt, 1, 0), jnp.moveaxis(Bm, 1, 0), jnp.moveaxis(Cm, 1, 0)))
    return jnp.moveaxis(ys, 0, 1), h_last


def hybrid_layer(x, c, pos, cache_k, cache_v, conv_buf, ssm_h, w_ada, b_ada, w_in, conv_w, conv_b,
                 dt_bias, a_log, d_skip, g_attn, g_ssm, w_out, w_up, w_down):
    fresh = cache_k is None
    b, T, _ = x.shape
    G, Hg = SSM_GROUPS, N_SSM_HEADS // SSM_GROUPS
    sh1, sc1, g1, sh2, sc2, g2 = jnp.split((jax.nn.silu(c) @ w_ada + b_ada)[:, None, :], N_MOD, axis=-1)

    h = modulate(x, sh1, sc1)
    q, k, v, z, xbc, dt_raw = jnp.split(h @ w_in, SPLIT_POINTS, axis=-1)
    q = rope(q.reshape(b, T, N_ATTN_HEADS, HEAD_DIM), pos) * (HEAD_DIM ** -0.5)
    k = rope(k.reshape(b, T, N_ATTN_HEADS, HEAD_DIM), pos)
    v = v.reshape(b, T, N_ATTN_HEADS, HEAD_DIM)
    if fresh:
        k_all, v_all = k, v
        parts = [dilated_band_attention(q, k, v, d, w // d) for w, d in DILATED_CONFIGS]
        xbc_ext = jnp.pad(xbc, ((0, 0), (CONV_WIDTH - 1, 0), (0, 0)))
        h0 = jnp.zeros((b, G, Hg, SSM_HEAD_DIM, D_STATE), jnp.float32)
    else:
        k_all = jnp.concatenate([cache_k.astype(k.dtype), k], axis=1)
        v_all = jnp.concatenate([cache_v.astype(v.dtype), v], axis=1)
        parts = [dilated_cached_attention(q, k_all, v_all, d, w // d) for w, d in DILATED_CONFIGS]
        xbc_ext = jnp.concatenate([conv_buf.astype(xbc.dtype), xbc], axis=1)
        h0 = ssm_h.astype(jnp.float32).reshape(b, G, Hg, SSM_HEAD_DIM, D_STATE)
    o_parts = jnp.stack([o for o, _ in parts])
    wts = jax.nn.softmax(jnp.stack([l for _, l in parts]), axis=0)
    attn = jnp.einsum('cbthd,cbth->bthd', o_parts, wts).reshape(b, T, D_ATTN).astype(x.dtype)

    xbc_c = causal_dwconv(xbc_ext, conv_w, conv_b)
    xs, Bm, Cm = jnp.split(xbc_c.astype(jnp.float32), [D_SSM, D_SSM + G * D_STATE], axis=-1)
    xs = xs.reshape(b, T, G, Hg, SSM_HEAD_DIM)
    Bm = Bm.reshape(b, T, G, D_STATE)
    Cm = Cm.reshape(b, T, G, D_STATE)
    dt = jax.nn.softplus(dt_raw.astype(jnp.float32) + dt_bias.astype(jnp.float32)).reshape(b, T, G, Hg)
    A = -jnp.exp(a_log.astype(jnp.float32)).reshape(G, Hg)
    if fresh:
        y, h_last = ssd_chunked(xs, dt, A, Bm, Cm, h0)
    else:
        y, h_last = ssd_recurrent(xs, dt, A, Bm, Cm, h0)
    y = y + d_skip.astype(jnp.float32).reshape(G, Hg)[:, :, None] * xs
    y = y.reshape(b, T, D_SSM) * jax.nn.silu(z.astype(jnp.float32))
    y = rms_norm(y.reshape(b, T, G, D_SSM // G)).reshape(b, T, D_SSM) * g_ssm

    mix = jnp.concatenate([rms_norm(attn) * g_attn, y.astype(x.dtype)], axis=-1) @ w_out
    x = x + g1 * mix

    u = jnp.square(jax.nn.relu(modulate(x, sh2, sc2) @ w_up))
    x = x + g2 * (u @ w_down)

    win = min(MAX_WINDOW, k_all.shape[1])
    return (x, k_all[:, -win:], v_all[:, -win:], xbc_ext[:, -(CONV_WIDTH - 1):],
            h_last.reshape(b, N_SSM_HEADS, SSM_HEAD_DIM, D_STATE))


def setup_inputs(seed: int = 0) -> dict:
    key = jax.random.key(seed)
    ks = jax.random.split(key, 24)
    f32 = jnp.float32
    win_buf = min(MAX_WINDOW, PAST_LEN)

    def nrm(k, shape, scale):
        return jax.random.normal(k, shape, f32) * scale

    dt0 = jnp.exp(jax.random.uniform(ks[12], (DEPTH, N_SSM_HEADS), f32, math.log(1e-3), math.log(1e-1)))
    return {
        'x_prompt': nrm(ks[0], (BATCH, SEQ, D_MODEL), 1.0),
        'x_sample': nrm(ks[1], (DEC_BATCH, DEC_SEQ, D_MODEL), 1.0),
        'c_prompt': nrm(ks[2], (BATCH, D_MODEL), 1.0),
        'c_sample': nrm(ks[3], (DEC_BATCH, D_MODEL), 1.0),
        'cache_k_win': nrm(ks[4], (DEPTH, DEC_BATCH, win_buf, N_ATTN_HEADS, HEAD_DIM), 1.0),
        'cache_v_win': nrm(ks[5], (DEPTH, DEC_BATCH, win_buf, N_ATTN_HEADS, HEAD_DIM), 1.0),
        'state_conv': nrm(ks[6], (DEPTH, DEC_BATCH, CONV_WIDTH - 1, D_CONV), 1.0),
        'state_ssm': nrm(ks[7], (DEPTH, DEC_BATCH, N_SSM_HEADS, SSM_HEAD_DIM, D_STATE), 0.1),
        'w_ada': nrm(ks[8], (DEPTH, D_MODEL, N_MOD * D_MODEL), 0.5 * D_MODEL ** -0.5),
        'b_ada': nrm(ks[9], (DEPTH, N_MOD * D_MODEL), 0.02),
        'w_in': nrm(ks[10], (DEPTH, D_MODEL, D_IN), D_MODEL ** -0.5),
        'conv_w': nrm(ks[11], (DEPTH, CONV_WIDTH, D_CONV), CONV_WIDTH ** -0.5),
        'conv_b': nrm(ks[13], (DEPTH, D_CONV), 0.02),
        'dt_bias': dt0 + jnp.log(-jnp.expm1(-dt0)),
        'a_log': jnp.log(jax.random.uniform(ks[14], (DEPTH, N_SSM_HEADS), f32, 1.0, 16.0)),
        'd_skip': 1.0 + nrm(ks[15], (DEPTH, N_SSM_HEADS), 0.1),
        'g_attn': 1.0 + nrm(ks[16], (DEPTH, D_ATTN), 0.1),
        'g_ssm': 1.0 + nrm(ks[17], (DEPTH, D_SSM), 0.1),
        'w_out': nrm(ks[18], (DEPTH, D_MIX, D_MODEL), D_MIX ** -0.5),
        'w_up': nrm(ks[19], (DEPTH, D_MODEL, D_FF), D_MODEL ** -0.5),
        'w_down': nrm(ks[20], (DEPTH, D_FF, D_MODEL), D_FF ** -0.5),
        'g_final': 1.0 + nrm(ks[21], (D_MODEL,), 0.1),
    }


def reference(x_prompt, x_sample, c_prompt, c_sample, cache_k_win, cache_v_win, state_conv, state_ssm,
              w_ada, b_ada, w_in, conv_w, conv_b, dt_bias, a_log, d_skip, g_attn, g_ssm, w_out, w_up, w_down, g_final):
    pos_p = jnp.arange(x_prompt.shape[1], dtype=jnp.int32)
    pos_s = PAST_LEN + jnp.arange(x_sample.shape[1], dtype=jnp.int32)
    xp, xs = x_prompt, x_sample
    kp_l, vp_l, cp_l, hp_l, ks_l, vs_l, cs_l, hs_l = [], [], [], [], [], [], [], []
    for layer in range(DEPTH):
        lw = (w_ada[layer], b_ada[layer], w_in[layer], conv_w[layer], conv_b[layer], dt_bias[layer], a_log[layer],
              d_skip[layer], g_attn[layer], g_ssm[layer], w_out[layer], w_up[layer], w_down[layer])
        xp, kp, vp, cp, hp = hybrid_layer(xp, c_prompt, pos_p, None, None, None, None, *lw)
        xs, kn, vn, cn, hn = hybrid_layer(xs, c_sample, pos_s, cache_k_win[layer], cache_v_win[layer],
                                          state_conv[layer], state_ssm[layer], *lw)
        kp_l.append(kp); vp_l.append(vp); cp_l.append(cp); hp_l.append(hp)
        ks_l.append(kn); vs_l.append(vn); cs_l.append(cn); hs_l.append(hn)
    y_prompt = rms_norm(xp) * g_final
    y_sample = rms_norm(xs) * g_final
    return (y_prompt, y_sample, jnp.stack(kp_l), jnp.stack(vp_l), jnp.stack(cp_l), jnp.stack(hp_l),
            jnp.stack(ks_l), jnp.stack(vs_l), jnp.stack(cs_l), jnp.stack(hs_l))
```

```python
import functools
import math

import jax
import jax.numpy as jnp
from jax import lax
from jax.experimental import pallas as pl
from jax.experimental.pallas import tpu as pltpu

F32 = jnp.float32
BF16 = jnp.bfloat16

HEAD_DIM = 64
N_HEADS = 16
D_MODEL = 1024
D_ATTN = 1024
D_SSM = 1024
SSM_GROUPS = 4
D_STATE = 128
CONV_WIDTH = 4
D_CONV = D_SSM + 2 * SSM_GROUPS * D_STATE
D_MAIN = 3 * D_ATTN + D_SSM + D_CONV
DILATED_CONFIGS = ((128, 1), (512, 4), (2048, 16))
MAX_WINDOW = 2048
BLK = 128
ROPE_THETA = 10000.0
NEG_INF = -1e30
EPS = 1e-6
LANES = 128
VMEM_LIMIT = 56 * 1024 * 1024
PAST_LEN = 8192


def _dot(a, b):
    return jnp.dot(a, b, preferred_element_type=F32)


def _dot_nt(a, b):
    return lax.dot_general(a, b, (((1,), (1,)), ((), ())), preferred_element_type=F32)


def _dot_tn(a, b):
    return lax.dot_general(a, b, (((0,), (0,)), ((), ())), preferred_element_type=F32)


def _split3(x):
    hi = x.astype(BF16)
    r = x - hi.astype(F32)
    mid = r.astype(BF16)
    lo = (r - mid.astype(F32)).astype(BF16)
    return hi, mid, lo


def _rms(x):
    return x * lax.rsqrt(jnp.mean(x * x, axis=-1, keepdims=True) + EPS)


def _silu(x):
    return x / (1.0 + jnp.exp(-x))


def _softplus(x):
    return jnp.maximum(x, 0.0) + jnp.log1p(jnp.exp(-jnp.abs(x)))


def _rope_chunk(t, cos, sin_signed, lower):
    rot = jnp.where(lower, pltpu.roll(t, 96, 1), pltpu.roll(t, 32, 1))
    return t * cos + rot * sin_signed


def _const_spec(shape):
    n = len(shape)
    return pl.BlockSpec(shape, lambda *_: (0,) * n, pipeline_mode=pl.Buffered(1))


def _ada_kernel(c_ref, w_ref, b_ref, o_ref):
    c = _silu(c_ref[...]).astype(BF16)
    o_ref[...] = _dot(c, w_ref[...].astype(BF16)) + b_ref[...]


def _ada(c_all, w_ada, b_ada):
    m, d = c_all.shape
    n = w_ada.shape[1]
    tn = 1024
    return pl.pallas_call(
        _ada_kernel,
        grid=(n // tn,),
        in_specs=[pl.BlockSpec((m, d), lambda j: (0, 0)),
                  pl.BlockSpec((d, tn), lambda j: (0, j)),
                  pl.BlockSpec((1, tn), lambda j: (0, j))],
        out_specs=pl.BlockSpec((m, tn), lambda j: (0, j)),
        out_shape=jax.ShapeDtypeStruct((m, n), F32),
        compiler_params=pltpu.CompilerParams(dimension_semantics=("arbitrary",), vmem_limit_bytes=VMEM_LIMIT),
        name="ada_mod",
    )(c_all, w_ada, b_ada)


def _inproj_prompt_kernel(x_ref, mod_ref, w_ref, wdt_ref, cos_ref, sin_ref, convw_ref, convb_ref, dtb_ref,
                          q_ref, k_ref, v_ref, z_ref, xbc_ref, dt_ref, dtT_ref, kwin_ref, vwin_ref, convst_ref,
                          buf_ref, *, tm, n_tiles):
    i = pl.program_id(1)
    x = x_ref[0]
    h = _rms(x) * (1.0 + mod_ref[0, 1:2, :]) + mod_ref[0, 0:1, :]
    hb = h.astype(BF16)
    cos = cos_ref[...]
    sin = sin_ref[...]
    lane = lax.broadcasted_iota(jnp.int32, (tm, LANES), 1)
    lower = (lane % HEAD_DIM) < (HEAD_DIM // 2)

    for c in range(D_ATTN // LANES):
        sl = slice(c * LANES, (c + 1) * LANES)
        qc = _rope_chunk(_dot(hb, w_ref[:, c * LANES:(c + 1) * LANES]), cos, sin, lower)
        q_ref[0, :, sl] = (qc * (HEAD_DIM ** -0.5)).astype(BF16)
        kc = _rope_chunk(_dot(hb, w_ref[:, D_ATTN + c * LANES:D_ATTN + (c + 1) * LANES]), cos, sin, lower)
        k_ref[0, :, sl] = kc.astype(BF16)
        kwin_ref[0, :, sl] = kc
    vv = _dot(hb, w_ref[:, 2 * D_ATTN:3 * D_ATTN])
    v_ref[0] = vv.astype(BF16)
    vwin_ref[0] = vv
    z_ref[0] = _dot(hb, w_ref[:, 3 * D_ATTN:3 * D_ATTN + D_SSM]).astype(BF16)

    @pl.when(i == 0)
    def _():
        buf_ref[0:8, :] = jnp.zeros((8, D_CONV), F32)

    base = 3 * D_ATTN + D_SSM
    cw = 512
    for cc in range(D_CONV // cw):
        sl = slice(cc * cw, (cc + 1) * cw)
        buf_ref[8:8 + tm, sl] = _dot(hb, w_ref[:, base + cc * cw:base + (cc + 1) * cw])
        acc = jnp.broadcast_to(convb_ref[:, sl], (tm, cw))
        for w in range(CONV_WIDTH):
            acc = acc + convw_ref[w:w + 1, sl] * buf_ref[5 + w:5 + w + tm, sl]
        xbc_ref[0, :, sl] = _silu(acc).astype(BF16)

    @pl.when(i == n_tiles - 1)
    def _():
        convst_ref[0] = buf_ref[tm + 5:tm + 8, :]

    buf_ref[0:8, :] = buf_ref[tm:tm + 8, :]

    dt = _softplus(_dot(hb, wdt_ref[...]) + dtb_ref[...])
    dt_ref[0] = dt[:, :N_HEADS]
    dtT_ref[0] = dt.T[:N_HEADS, :]


def _inproj_prompt(x, mod, w_main, w_dt, cos, sin, conv_w, conv_b, dtb_pad, tm=512):
    b, s, d = x.shape
    n_tiles = s // tm
    win = min(MAX_WINDOW, s)
    first_win_tile = (s - win) // tm
    row = lambda bi, i: (bi, i, 0)
    winrow = lambda bi, i: (bi, jnp.maximum(i - first_win_tile, 0), 0)
    kern = functools.partial(_inproj_prompt_kernel, tm=tm, n_tiles=n_tiles)
    out_shape = [
        jax.ShapeDtypeStruct((b, s, D_ATTN), BF16),
        jax.ShapeDtypeStruct((b, s, D_ATTN), BF16),
        jax.ShapeDtypeStruct((b, s, D_ATTN), BF16),
        jax.ShapeDtypeStruct((b, s, D_SSM), BF16),
        jax.ShapeDtypeStruct((b, s, D_CONV), BF16),
        jax.ShapeDtypeStruct((b, s, N_HEADS), F32),
        jax.ShapeDtypeStruct((b, N_HEADS, s), F32),
        jax.ShapeDtypeStruct((b, win, D_ATTN), F32),
        jax.ShapeDtypeStruct((b, win, D_ATTN), F32),
        jax.ShapeDtypeStruct((b, CONV_WIDTH - 1, D_CONV), F32),
    ]
    out_specs = [
        pl.BlockSpec((1, tm, D_ATTN), row),
        pl.BlockSpec((1, tm, D_ATTN), row),
        pl.BlockSpec((1, tm, D_ATTN), row),
        pl.BlockSpec((1, tm, D_SSM), row),
        pl.BlockSpec((1, tm, D_CONV), row),
        pl.BlockSpec((1, tm, N_HEADS), row),
        pl.BlockSpec((1, N_HEADS, tm), lambda bi, i: (bi, 0, i)),
        pl.BlockSpec((1, tm, D_ATTN), winrow),
        pl.BlockSpec((1, tm, D_ATTN), winrow),
        pl.BlockSpec((1, CONV_WIDTH - 1, D_CONV), lambda bi, i: (bi, 0, 0)),
    ]
    in_specs = [
        pl.BlockSpec((1, tm, d), row),
        pl.BlockSpec((1, 6, d), lambda bi, i: (bi, 0, 0)),
        _const_spec(w_main.shape),
        _const_spec(w_dt.shape),
        pl.BlockSpec((tm, LANES), lambda bi, i: (i, 0)),
        pl.BlockSpec((tm, LANES), lambda bi, i: (i, 0)),
        _const_spec(conv_w.shape),
        _const_spec(conv_b.shape),
        _const_spec(dtb_pad.shape),
    ]
    return pl.pallas_call(
        kern, grid=(b, n_tiles), in_specs=in_specs, out_specs=out_specs, out_shape=out_shape,
        scratch_shapes=[pltpu.VMEM((tm + 8, D_CONV), F32)],
        compiler_params=pltpu.CompilerParams(dimension_semantics=("arbitrary", "arbitrary"),
                                             vmem_limit_bytes=VMEM_LIMIT),
        name="inproj_prompt",
    )(x, mod, w_main, w_dt, cos, sin, conv_w, conv_b, dtb_pad)


def _ssd_kernel(xbc_ref, z_ref, dt_ref, dtT_ref, alog_ref, alogT_ref, dskip_ref, gssm_ref,
                y_ref, hlast_ref, state_ref, ybuf_ref, *, n_chunks):
    c = pl.program_id(1)

    @pl.when(c == 0)
    def _():
        state_ref[...] = jnp.zeros_like(state_ref)

    a_row = -jnp.exp(alog_ref[...])
    a_col = -jnp.exp(alogT_ref[...])
    dt = dt_ref[0]
    dtT = dtT_ref[0]
    row = lax.broadcasted_iota(jnp.int32, (BLK, BLK), 0)
    col = lax.broadcasted_iota(jnp.int32, (BLK, BLK), 1)
    tril = row >= col
    tri_l = tril.astype(BF16)
    tri_u = (row <= col).astype(BF16)
    lo = col < HEAD_DIM

    cum = sum(_dot(tri_l, p) for p in _split3(dt * a_row))
    cumT = sum(_dot(p, tri_u) for p in _split3(dtT * a_col))
    ecum = jnp.exp(cum)
    dte = jnp.exp(cum[BLK - 1:BLK, :] - cum) * dt
    ecl = jnp.exp(cumT[:, BLK - 1:BLK])

    for g in range(SSM_GROUPS):
        bg = xbc_ref[0, :, D_SSM + g * D_STATE:D_SSM + (g + 1) * D_STATE]
        cg = xbc_ref[0, :, D_SSM + (SSM_GROUPS + g) * D_STATE:D_SSM + (SSM_GROUPS + g + 1) * D_STATE]
        cb = _dot_nt(cg, bg)
        for jj in range(2):
            j = 2 * g + jj
            h0, h1 = 2 * j, 2 * j + 1
            xp = xbc_ref[0, :, j * LANES:(j + 1) * LANES]
            y_pair = jnp.zeros((BLK, LANES), F32)
            for hm, h in ((lo, h0), (~lo, h1)):
                seg = cum[:, h:h + 1] - cumT[h:h + 1, :]
                dec = jnp.where(tril, jnp.exp(seg), 0.0)
                m = (cb * dec * dtT[h:h + 1, :]).astype(BF16)
                y_pair = y_pair + _dot(m, jnp.where(hm, xp, jnp.zeros_like(xp)))
            s_pair = state_ref[j]
            e_pair = jnp.where(lo, ecum[:, h0:h0 + 1], ecum[:, h1:h1 + 1])
            y_pair = y_pair + _dot_nt(cg, s_pair.astype(BF16)) * e_pair
            w_pair = jnp.where(lo, dte[:, h0:h0 + 1], dte[:, h1:h1 + 1])
            xf = xp.astype(F32)
            upd = _dot_tn((xf * w_pair).astype(BF16), bg)
            scale = jnp.where(row < HEAD_DIM, ecl[h0:h0 + 1, :], ecl[h1:h1 + 1, :])
            state_ref[j] = s_pair * scale + upd
            sl = slice(j * LANES, (j + 1) * LANES)
            ybuf_ref[:, sl] = y_pair + dskip_ref[:, sl] * xf

    gw = D_SSM // SSM_GROUPS
    for g in range(SSM_GROUPS):
        sl = slice(g * gw, (g + 1) * gw)
        y = ybuf_ref[:, sl] * _silu(z_ref[0, :, sl].astype(F32))
        y_ref[0, :, sl] = (_rms(y) * gssm_ref[:, sl]).astype(BF16)

    @pl.when(c == n_chunks - 1)
    def _():
        hlast_ref[0] = state_ref[...]


def _ssd_prompt(xbc, z, dt, dtT, a_log, a_logT, dskip_lane, g_ssm):
    b, s, _ = xbc.shape
    n_chunks = s // BLK
    row = lambda bi, c: (bi, c, 0)
    return pl.pallas_call(
        functools.partial(_ssd_kernel, n_chunks=n_chunks),
        grid=(b, n_chunks),
        in_specs=[
            pl.BlockSpec((1, BLK, D_CONV), row),
            pl.BlockSpec((1, BLK, D_SSM), row),
            pl.BlockSpec((1, BLK, N_HEADS), row),
            pl.BlockSpec((1, N_HEADS, BLK), lambda bi, c: (bi, 0, c)),
            pl.BlockSpec((1, N_HEADS), lambda bi, c: (0, 0)),
            pl.BlockSpec((N_HEADS, 1), lambda bi, c: (0, 0)),
            pl.BlockSpec((1, D_SSM), lambda bi, c: (0, 0)),
            pl.BlockSpec((1, D_SSM), lambda bi, c: (0, 0)),
        ],
        out_specs=[
            pl.BlockSpec((1, BLK, D_SSM), row),
            pl.BlockSpec((1, N_HEADS // 2, LANES, D_STATE), lambda bi, c: (bi, 0, 0, 0)),
        ],
        out_shape=[
            jax.ShapeDtypeStruct((b, s, D_SSM), BF16),
            jax.ShapeDtypeStruct((b, N_HEADS // 2, LANES, D_STATE), F32),
        ],
        scratch_shapes=[pltpu.VMEM((N_HEADS // 2, LANES, D_STATE), F32), pltpu.VMEM((BLK, D_SSM), F32)],
        compiler_params=pltpu.CompilerParams(dimension_semantics=("arbitrary", "arbitrary"),
                                             vmem_limit_bytes=VMEM_LIMIT),
        name="ssd_prompt",
    )(xbc, z, dt, dtT, a_log, a_logT, dskip_lane, g_ssm)


def _attn_kernel(q_ref, kp_ref, kc_ref, vp_ref, vc_ref, o_ref, lse_ref):
    n = pl.program_id(2)
    row = lax.broadcasted_iota(jnp.int32, (BLK, BLK), 0)
    col = lax.broadcasted_iota(jnp.int32, (BLK, BLK), 1)
    mask_cur = col <= row
    mask_prev = (col >= row) & (n > 0)
    lo = col < HEAD_DIM
    lse_acc = jnp.zeros((BLK, LANES), F32)
    for j in range(N_HEADS // 2):
        sl = slice(j * LANES, (j + 1) * LANES)
        qp = q_ref[0, :, sl]
        kp = kp_ref[0, :, sl]
        kc = kc_ref[0, :, sl]
        vp = vp_ref[0, :, sl]
        vc = vc_ref[0, :, sl]
        o_pair = jnp.zeros((BLK, LANES), F32)
        for hh, hm in ((0, lo), (1, ~lo)):
            qm = jnp.where(hm, qp, jnp.zeros_like(qp))
            sp = jnp.where(mask_prev, _dot_nt(qm, kp), NEG_INF)
            sc = jnp.where(mask_cur, _dot_nt(qm, kc), NEG_INF)
            m = jnp.max(jnp.maximum(sp, sc), axis=1, keepdims=True)
            pp = jnp.exp(sp - m)
            pc = jnp.exp(sc - m)
            l = jnp.sum(pp + pc, axis=1, keepdims=True)
            o = (_dot(pp.astype(BF16), jnp.where(hm, vp, jnp.zeros_like(vp)))
                 + _dot(pc.astype(BF16), jnp.where(hm, vc, jnp.zeros_like(vc))))
            o_pair = o_pair + o / l
            lse_acc = jnp.where(col == 2 * j + hh, m + jnp.log(l), lse_acc)
        o_ref[0, :, sl] = o_pair.astype(o_ref.dtype)
    lse_ref[0, 0] = lse_acc[:, :N_HEADS]


def _attn_prompt(q, k, v, dil):
    b, s, d = q.shape
    ls = s // dil
    nb = ls // BLK
    view = lambda t: t.reshape(b, ls, dil * d)
    cur = lambda bi, r, n: (bi, n, r)
    prev = lambda bi, r, n: (bi, jnp.maximum(n - 1, 0), r)
    blk = (1, BLK, d)
    o, lse = pl.pallas_call(
        _attn_kernel,
        grid=(b, dil, nb),
        in_specs=[pl.BlockSpec(blk, cur), pl.BlockSpec(blk, prev), pl.BlockSpec(blk, cur),
                  pl.BlockSpec(blk, prev), pl.BlockSpec(blk, cur)],
        out_specs=[pl.BlockSpec(blk, cur), pl.BlockSpec((1, 1, BLK, N_HEADS), lambda bi, r, n: (bi, r, n, 0))],
        out_shape=[jax.ShapeDtypeStruct((b, ls, dil * d), BF16),
                   jax.ShapeDtypeStruct((b, dil, ls, N_HEADS), F32)],
        compiler_params=pltpu.CompilerParams(dimension_semantics=("arbitrary", "arbitrary", "arbitrary"),
                                             vmem_limit_bytes=VMEM_LIMIT),
        name=f"attn_prompt_d{dil}",
    )(view(q), view(k), view(k), view(v), view(v))
    return o.reshape(b, s, d), lse.transpose(0, 2, 1, 3).reshape(b, s, N_HEADS)


def _expand_heads(w, lane_lo):
    rows = w.shape[0]
    out = []
    for j in range(N_HEADS // 2):
        a = jnp.broadcast_to(w[:, 2 * j:2 * j + 1], (rows, LANES))
        bb = jnp.broadcast_to(w[:, 2 * j + 1:2 * j + 2], (rows, LANES))
        out.append(jnp.where(lane_lo, a, bb))
    return out


def _tail_kernel(*refs, n_cfg, per_row_mod, tm):
    x_ref, mod_ref = refs[0], refs[1]
    o_refs = refs[2:2 + n_cfg]
    lse_refs = refs[2 + n_cfg:2 + 2 * n_cfg] if n_cfg > 1 else ()
    k0 = 2 + (2 * n_cfg if n_cfg > 1 else n_cfg)
    yssm_ref, gattn_ref, gfin_ref, wout_ref, wup_ref, wdown_ref, y_ref, attn_buf = refs[k0:k0 + 8]

    if per_row_mod:
        mod = lambda k: mod_ref[k]
    else:
        mod = lambda k: mod_ref[0, k:k + 1, :]
    x = x_ref[...].reshape(tm, D_MODEL)

    if n_cfg > 1:
        lses = [r[...].reshape(tm, N_HEADS) for r in lse_refs]
        mx = functools.reduce(jnp.maximum, lses)
        es = [jnp.exp(l - mx) for l in lses]
        den = functools.reduce(lambda a, b_: a + b_, es)
        lane_lo = lax.broadcasted_iota(jnp.int32, (tm, LANES), 1) < HEAD_DIM
        ws = [_expand_heads(e / den, lane_lo) for e in es]
        for j in range(N_HEADS // 2):
            sl = slice(j * LANES, (j + 1) * LANES)
            acc = jnp.zeros((tm, LANES), F32)
            for ci in range(n_cfg):
                acc = acc + o_refs[ci][0, :, sl].astype(F32) * ws[ci][j]
            attn_buf[:, sl] = acc
        attn = attn_buf[...]
    else:
        attn = o_refs[0][...].T
    attn_n = (_rms(attn) * gattn_ref[...]).astype(BF16)
    mix = _dot(attn_n, wout_ref[0:D_ATTN, :]) + _dot(yssm_ref[...].reshape(tm, D_SSM), wout_ref[D_ATTN:, :])
    x1 = x + mod(2) * mix
    h2 = (_rms(x1) * (1.0 + mod(4)) + mod(3)).astype(BF16)
    d_ff = wup_ref.shape[1]
    cw = 1024
    acc = jnp.zeros((tm, D_MODEL), F32)
    for cc in range(d_ff // cw):
        u = jnp.maximum(_dot(h2, wup_ref[:, cc * cw:(cc + 1) * cw]), 0.0)
        acc = acc + _dot((u * u).astype(BF16), wdown_ref[cc * cw:(cc + 1) * cw, :])
    x2 = x1 + mod(5) * acc
    y_ref[...] = (_rms(x2) * gfin_ref[...]).reshape(y_ref.shape)


def _tail_prompt(x, mod, os_, lses, y_ssm, g_attn, g_final, w_out, w_up, w_down, tm=512):
    b, s, d = x.shape
    row = lambda bi, i: (bi, i, 0)
    n_cfg = len(os_)
    in_specs = ([pl.BlockSpec((1, tm, d), row), pl.BlockSpec((1, 6, d), lambda bi, i: (bi, 0, 0))]
                + [pl.BlockSpec((1, tm, D_ATTN), row)] * n_cfg
                + [pl.BlockSpec((1, tm, N_HEADS), row)] * n_cfg
                + [pl.BlockSpec((1, tm, D_SSM), row),
                   _const_spec(g_attn.shape), _const_spec(g_final.shape),
                   _const_spec(w_out.shape), _const_spec(w_up.shape), _const_spec(w_down.shape)])
    return pl.pallas_call(
        functools.partial(_tail_kernel, n_cfg=n_cfg, per_row_mod=False, tm=tm),
        grid=(b, s // tm), in_specs=in_specs,
        out_specs=pl.BlockSpec((1, tm, d), row),
        out_shape=jax.ShapeDtypeStruct((b, s, d), F32),
        scratch_shapes=[pltpu.VMEM((tm, D_ATTN), F32)],
        compiler_params=pltpu.CompilerParams(dimension_semantics=("arbitrary", "arbitrary"),
                                             vmem_limit_bytes=VMEM_LIMIT),
        name="tail_prompt",
    )(x, mod, *os_, *lses, y_ssm, g_attn, g_final, w_out, w_up, w_down)


def _tail_sample(x, mod_t, attn, y_ssm, g_attn, g_final, w_out, w_up, w_down):
    m, d = x.shape
    full = lambda shape: pl.BlockSpec(shape, lambda i: (0,) * len(shape))
    in_specs = [full((m, d)), full(mod_t.shape), full((D_ATTN, m)), full((m, D_SSM)),
                full(g_attn.shape), full(g_final.shape), full(w_out.shape), full(w_up.shape), full(w_down.shape)]
    return pl.pallas_call(
        functools.partial(_tail_kernel, n_cfg=1, per_row_mod=True, tm=m),
        grid=(1,), in_specs=in_specs,
        out_specs=full((m, d)),
        out_shape=jax.ShapeDtypeStruct((m, d), F32),
        scratch_shapes=[pltpu.VMEM((m, D_ATTN), F32)],
        compiler_params=pltpu.CompilerParams(dimension_semantics=("arbitrary",), vmem_limit_bytes=VMEM_LIMIT),
        name="tail_sample",
    )(x, mod_t, attn, y_ssm, g_attn, g_final, w_out, w_up, w_down)


def _inproj_sample_kernel(x_ref, mod_ref, w_ref, wdt_ref, cos_ref, sin_ref, convst_ref, convw_ref, convb_ref,
                          dtb_ref, qT_ref, kT_ref, vT_ref, z_ref, xbc_ref, xT_ref, dtT_ref, convout_ref):
    m = x_ref.shape[0]
    h = _rms(x_ref[...]) * (1.0 + mod_ref[1]) + mod_ref[0]
    hb = h.astype(BF16)
    cos = cos_ref[...]
    sin = sin_ref[...]
    lane = lax.broadcasted_iota(jnp.int32, (m, LANES), 1)
    lower = (lane % HEAD_DIM) < (HEAD_DIM // 2)
    for c in range(D_ATTN // LANES):
        sl = slice(c * LANES, (c + 1) * LANES)
        qc = _rope_chunk(_dot(hb, w_ref[:, c * LANES:(c + 1) * LANES]), cos, sin, lower)
        qT_ref[sl, :] = (qc * (HEAD_DIM ** -0.5)).T
        kc = _rope_chunk(_dot(hb, w_ref[:, D_ATTN + c * LANES:D_ATTN + (c + 1) * LANES]), cos, sin, lower)
        kT_ref[sl, :] = kc.T
        vT_ref[sl, :] = _dot(hb, w_ref[:, 2 * D_ATTN + c * LANES:2 * D_ATTN + (c + 1) * LANES]).T
    z_ref[...] = _dot(hb, w_ref[:, 3 * D_ATTN:3 * D_ATTN + D_SSM])
    base = 3 * D_ATTN + D_SSM
    xbc = _dot(hb, w_ref[:, base:base + D_CONV])
    acc = convb_ref[...] + convw_ref[CONV_WIDTH - 1:CONV_WIDTH, :] * xbc
    for w in range(CONV_WIDTH - 1):
        acc = acc + convw_ref[w:w + 1, :] * convst_ref[w]
    xc = _silu(acc)
    xbc_ref[...] = xc
    xT_ref[...] = xc[:, :D_SSM].T
    for w in range(CONV_WIDTH - 2):
        convout_ref[w] = convst_ref[w + 1]
    convout_ref[CONV_WIDTH - 2] = xbc
    dt = _softplus(_dot(hb, wdt_ref[...]) + dtb_ref[...])
    dtT_ref[...] = dt.T[:N_HEADS, :]


def _inproj_sample(x, mod_t, w_main, w_dt, cos, sin, conv_state, conv_w, conv_b, dtb_pad):
    m, d = x.shape
    full = lambda shape: pl.BlockSpec(shape, lambda i: (0,) * len(shape))
    ins = (x, mod_t, w_main, w_dt, cos, sin, conv_state, conv_w, conv_b, dtb_pad)
    out_shape = [
        jax.ShapeDtypeStruct((D_ATTN, m), F32), jax.ShapeDtypeStruct((D_ATTN, m), F32),
        jax.ShapeDtypeStruct((D_ATTN, m), F32), jax.ShapeDtypeStruct((m, D_SSM), F32),
        jax.ShapeDtypeStruct((m, D_CONV), F32), jax.ShapeDtypeStruct((D_SSM, m), F32),
        jax.ShapeDtypeStruct((N_HEADS, m), F32), jax.ShapeDtypeStruct((CONV_WIDTH - 1, m, D_CONV), F32),
    ]
    return pl.pallas_call(
        _inproj_sample_kernel, grid=(1,),
        in_specs=[full(a.shape) for a in ins],
        out_specs=[full(o.shape) for o in out_shape],
        out_shape=out_shape,
        compiler_params=pltpu.CompilerParams(dimension_semantics=("arbitrary",), vmem_limit_bytes=VMEM_LIMIT),
        name="inproj_sample",
    )(*ins)


def _attn_sample_kernel(qT_ref, knT_ref, vnT_ref, kc_ref, vc_ref, oT_ref, ko_ref, vo_ref, *, hpb, win, nb):
    b = pl.program_id(1)
    sel = lax.broadcasted_iota(jnp.int32, (HEAD_DIM, nb), 1) == b
    pick = lambda ref, rs: jnp.sum(jnp.where(sel, ref[rs, :], 0.0), axis=1, keepdims=True)
    nt = win // LANES
    lane = lax.broadcasted_iota(jnp.int32, (HEAD_DIM, LANES), 1)
    keep = lane < LANES - 1
    lane1 = lax.broadcasted_iota(jnp.int32, (1, LANES), 1)
    n_cfg = float(len(DILATED_CONFIGS))

    @pl.when(b == 0)
    def _():
        oT_ref[...] = jnp.zeros_like(oT_ref)

    def mult(c):
        w = lane1 + c * LANES
        tot = jnp.zeros((1, LANES), F32)
        for wd, dil in DILATED_CONFIGS:
            tot = tot + ((w >= win - wd) & ((win - w) % dil == 0)).astype(F32)
        return tot

    mults = [mult(c) for c in range(nt)]

    for h in range(hpb):
        rs = slice(h * HEAD_DIM, (h + 1) * HEAD_DIM)
        qc = pick(qT_ref, rs)
        knc = pick(knT_ref, rs)
        vnc = pick(vnT_ref, rs)
        s_self = jnp.sum(qc * knc, axis=0, keepdims=True)
        s_tiles = []
        m = s_self
        for c in range(nt):
            sc = jnp.sum(kc_ref[0, h, :, c * LANES:(c + 1) * LANES] * qc, axis=0, keepdims=True)
            sc = jnp.where(mults[c] > 0, sc, NEG_INF)
            s_tiles.append(sc)
            m = jnp.maximum(m, jnp.max(sc, axis=1, keepdims=True))
        p_self = n_cfg * jnp.exp(s_self - m)
        l = p_self
        o = p_self * vnc
        for c in range(nt):
            p = mults[c] * jnp.exp(s_tiles[c] - m)
            l = l + jnp.sum(p, axis=1, keepdims=True)
            o = o + jnp.sum(vc_ref[0, h, :, c * LANES:(c + 1) * LANES] * p, axis=1, keepdims=True)
        oT_ref[rs, :] = jnp.where(sel, o / l, oT_ref[rs, :])
        for src, dst, newc in ((kc_ref, ko_ref, knc), (vc_ref, vo_ref, vnc)):
            nxt = jnp.broadcast_to(newc, (HEAD_DIM, LANES))
            for c in reversed(range(nt)):
                r = pltpu.roll(src[0, h, :, c * LANES:(c + 1) * LANES], LANES - 1, 1)
                dst[0, h, :, c * LANES:(c + 1) * LANES] = jnp.where(keep, r, nxt)
                nxt = r


def _attn_sample(qT, knT, vnT, cache_kT, cache_vT, hpb=8):
    nb, nh, hd, win = cache_kT.shape
    colspec = pl.BlockSpec((hpb * hd, nb), lambda g, b: (g, 0))
    cspec = pl.BlockSpec((1, hpb, hd, win), lambda g, b: (b, g, 0, 0))
    return pl.pallas_call(
        functools.partial(_attn_sample_kernel, hpb=hpb, win=win, nb=nb),
        grid=(nh // hpb, nb),
        in_specs=[colspec, colspec, colspec, cspec, cspec],
        out_specs=[colspec, cspec, cspec],
        out_shape=[jax.ShapeDtypeStruct((nh * hd, nb), F32),
                   jax.ShapeDtypeStruct(cache_kT.shape, F32), jax.ShapeDtypeStruct(cache_vT.shape, F32)],
        compiler_params=pltpu.CompilerParams(dimension_semantics=("arbitrary", "arbitrary"),
                                             vmem_limit_bytes=VMEM_LIMIT),
        name="attn_sample_shift",
    )(qT, knT, vnT, cache_kT, cache_vT)


def _ssm_sample_kernel(st_ref, xT_ref, b_ref, c_ref, dtT_ref, alogT_ref, new_ref, yT_ref, *, nb):
    h = pl.program_id(0)
    a_col = -jnp.exp(alogT_ref[...])
    dtT = dtT_ref[...]
    decT = jnp.exp(dtT * a_col)
    hsel = lax.broadcasted_iota(jnp.int32, (N_HEADS, nb), 0) == h
    dt_h = jnp.sum(jnp.where(hsel, dtT, 0.0), axis=0, keepdims=True)
    dec_h = jnp.sum(jnp.where(hsel, decT, 0.0), axis=0, keepdims=True)
    xdt = xT_ref[...] * dt_h
    lane = lax.broadcasted_iota(jnp.int32, (HEAD_DIM, nb), 1)
    y_acc = jnp.zeros((HEAD_DIM, nb), F32)
    for bi in range(nb):
        outer = xdt[:, bi:bi + 1] * b_ref[bi:bi + 1, :]
        new = st_ref[0, bi, 0] * dec_h[:, bi:bi + 1] + outer
        new_ref[0, bi, 0] = new
        ycol = jnp.sum(new * c_ref[bi:bi + 1, :], axis=1, keepdims=True)
        y_acc = jnp.where(lane == bi, ycol, y_acc)
    yT_ref[...] = y_acc


def _ssm_sample(state, xT, xbc, dtT, a_logT):
    _, nb, nh, hp, n = state.shape
    hg = nh // SSM_GROUPS
    b_blk0 = D_SSM // D_STATE
    return pl.pallas_call(
        functools.partial(_ssm_sample_kernel, nb=nb),
        grid=(nh,),
        in_specs=[pl.BlockSpec((1, nb, 1, hp, n), lambda h: (0, 0, h, 0, 0)),
                  pl.BlockSpec((hp, nb), lambda h: (h, 0)),
                  pl.BlockSpec((nb, D_STATE), lambda h: (0, b_blk0 + h // hg)),
                  pl.BlockSpec((nb, D_STATE), lambda h: (0, b_blk0 + SSM_GROUPS + h // hg)),
                  pl.BlockSpec((nh, nb), lambda h: (0, 0)),
                  pl.BlockSpec((nh, 1), lambda h: (0, 0))],
        out_specs=[pl.BlockSpec((1, nb, 1, hp, n), lambda h: (0, 0, h, 0, 0)),
                   pl.BlockSpec((hp, nb), lambda h: (h, 0))],
        out_shape=[jax.ShapeDtypeStruct(state.shape, F32), jax.ShapeDtypeStruct((nh * hp, nb), F32)],
        compiler_params=pltpu.CompilerParams(dimension_semantics=("arbitrary",), vmem_limit_bytes=VMEM_LIMIT),
        name="ssm_sample",
    )(state, xT, xbc, xbc, dtT, a_logT)


def _ssm_gate_kernel(yT_ref, xs_ref, z_ref, dskip_ref, gssm_ref, o_ref):
    y = yT_ref[...].T + dskip_ref[...] * xs_ref[...]
    y = y * _silu(z_ref[...])
    gw = D_SSM // SSM_GROUPS
    for g in range(SSM_GROUPS):
        sl = slice(g * gw, (g + 1) * gw)
        o_ref[:, sl] = (_rms(y[:, sl]) * gssm_ref[:, sl]).astype(BF16)


def _ssm_gate(yT, xbc, z, dskip_lane, g_ssm):
    m = z.shape[0]
    full = lambda shape: pl.BlockSpec(shape, lambda i: (0,) * len(shape))
    return pl.pallas_call(
        _ssm_gate_kernel, grid=(1,),
        in_specs=[full(yT.shape), pl.BlockSpec((m, D_SSM), lambda i: (0, 0)), full(z.shape),
                  full(dskip_lane.shape), full(g_ssm.shape)],
        out_specs=full((m, D_SSM)),
        out_shape=jax.ShapeDtypeStruct((m, D_SSM), BF16),
        compiler_params=pltpu.CompilerParams(dimension_semantics=("arbitrary",), vmem_limit_bytes=VMEM_LIMIT),
        name="ssm_gate_sample",
    )(yT, xbc, z, dskip_lane, g_ssm)


def _rope_tables(pos):
    half = HEAD_DIM // 2
    inv_freq = ROPE_THETA ** (-jnp.arange(half, dtype=F32) / half)
    ang = pos.astype(F32)[:, None] * inv_freq[None, :]
    cos = jnp.tile(jnp.cos(ang), (1, LANES // half))
    sin = jnp.tile(jnp.sin(ang), (1, LANES // half))
    sign = jnp.where((jnp.arange(LANES) % HEAD_DIM) < half, -1.0, 1.0).astype(F32)
    return cos, sin * sign[None, :]


def kernel(x_prompt, x_sample, c_prompt, c_sample, cache_k_win, cache_v_win, state_conv, state_ssm, w_ada, b_ada,
           w_in, conv_w, conv_b, dt_bias, a_log, d_skip, g_attn, g_ssm, w_out, w_up, w_down, g_final):
    depth = w_ada.shape[0]
    assert depth == 1, "single-layer trunk"
    b, s, d = x_prompt.shape
    nb, t_new, _ = x_sample.shape
    assert t_new == 1
    win = cache_k_win.shape[2]
    assert win == MAX_WINDOW and s % MAX_WINDOW == 0
    past_len = PAST_LEN
    layer = 0

    w_in_l = w_in[layer]
    w_main = w_in_l[:, :D_MAIN].astype(BF16)
    w_dt = jnp.pad(w_in_l[:, D_MAIN:], ((0, 0), (0, LANES - N_HEADS))).astype(BF16)
    dtb_pad = jnp.pad(dt_bias[layer].astype(F32), (0, LANES - N_HEADS))[None, :]
    w_out_b = w_out[layer].astype(BF16)
    w_up_b = w_up[layer].astype(BF16)
    w_down_b = w_down[layer].astype(BF16)
    conv_w_l = conv_w[layer]
    conv_b_l = conv_b[layer][None, :]
    a_log_row = a_log[layer][None, :]
    a_log_col = a_log[layer][:, None]
    dskip_lane = jnp.repeat(d_skip[layer].astype(F32), HEAD_DIM)[None, :]
    g_attn_l = g_attn[layer][None, :]
    g_ssm_l = g_ssm[layer][None, :]
    g_final_l = g_final[None, :]

    pad_rows = (-(nb + b)) % 8
    c_all = jnp.concatenate([c_sample, c_prompt, jnp.zeros((pad_rows, d), F32)], axis=0)
    mod = _ada(c_all, w_ada[layer], b_ada[layer][None, :])
    mod_s = mod[:nb].reshape(nb, 6, d).transpose(1, 0, 2)
    mod_p = mod[nb:nb + b].reshape(b, 6, d)

    cos_p, sin_p = _rope_tables(jnp.arange(s, dtype=jnp.int32))
    q, k, v, z, xbc, dt, dtT, k_win, v_win, conv_p = _inproj_prompt(
        x_prompt, mod_p, w_main, w_dt, cos_p, sin_p, conv_w_l, conv_b_l, dtb_pad)
    y_ssm, h_last = _ssd_prompt(xbc, z, dt, dtT, a_log_row, a_log_col, dskip_lane, g_ssm_l)
    parts = [_attn_prompt(q, k, v, dil) for _, dil in DILATED_CONFIGS]
    y_prompt = _tail_prompt(x_prompt, mod_p, [o for o, _ in parts], [l for _, l in parts], y_ssm,
                            g_attn_l, g_final_l, w_out_b, w_up_b, w_down_b)

    cos_s, sin_s = _rope_tables(past_len + jnp.arange(1, dtype=jnp.int32))
    qT_s, kT_s, vT_s, z_s, xbc_s, xT_s, dtT_s, conv_s = _inproj_sample(
        x_sample.reshape(nb, d), mod_s, w_main, w_dt, cos_s, sin_s, state_conv[layer].transpose(1, 0, 2),
        conv_w_l, conv_b_l, dtb_pad)
    attnT_s, k_winT_s, v_winT_s = _attn_sample(qT_s, kT_s, vT_s,
                                               cache_k_win[layer].transpose(0, 2, 3, 1),
                                               cache_v_win[layer].transpose(0, 2, 3, 1))
    ssm_s, yT_s = _ssm_sample(state_ssm[layer][None], xT_s, xbc_s, dtT_s, a_log_col)
    y_ssm_s = _ssm_gate(yT_s, xbc_s, z_s, dskip_lane, g_ssm_l)
    y_sample = _tail_sample(x_sample.reshape(nb, d), mod_s, attnT_s, y_ssm_s,
                            g_attn_l, g_final_l, w_out_b, w_up_b, w_down_b)

    hd = (N_HEADS, HEAD_DIM)
    return (y_prompt, y_sample.reshape(nb, 1, d),
            k_win.reshape(1, b, -1, *hd), v_win.reshape(1, b, -1, *hd),
            conv_p[None], h_last.reshape(1, b, N_HEADS, HEAD_DIM, D_STATE),
            k_winT_s.transpose(0, 3, 1, 2)[None], v_winT_s.transpose(0, 3, 1, 2)[None],
            conv_s.transpose(1, 0, 2)[None], ssm_s)
```

```python
import functools
import math

import jax
import jax.numpy as jnp
from jax import lax
from jax.experimental import pallas as pl
from jax.experimental.pallas import tpu as pltpu

F32 = jnp.float32
BF16 = jnp.bfloat16

HEAD_DIM = 64
N_HEADS = 16
D_MODEL = 1024
D_ATTN = 1024
D_SSM = 1024
SSM_GROUPS = 4
D_STATE = 128
CONV_WIDTH = 4
D_CONV = D_SSM + 2 * SSM_GROUPS * D_STATE
D_MAIN = 3 * D_ATTN + D_SSM + D_CONV
DILATED_CONFIGS = ((128, 1), (512, 4), (2048, 16))
MAX_WINDOW = 2048
BLK = 128
ROPE_THETA = 10000.0
NEG_INF = -1e30
EPS = 1e-6
LANES = 128
VMEM_LIMIT = 56 * 1024 * 1024
PAST_LEN = 8192


def _dot(a, b):
    return jnp.dot(a, b, preferred_element_type=F32)


def _dot_nt(a, b):
    return lax.dot_general(a, b, (((1,), (1,)), ((), ())), preferred_element_type=F32)


def _dot_tn(a, b):
    return lax.dot_general(a, b, (((0,), (0,)), ((), ())), preferred_element_type=F32)


def _split3(x):
    hi = x.astype(BF16)
    r = x - hi.astype(F32)
    mid = r.astype(BF16)
    lo = (r - mid.astype(F32)).astype(BF16)
    return hi, mid, lo


def _rms(x):
    return x * lax.rsqrt(jnp.mean(x * x, axis=-1, keepdims=True) + EPS)


def _silu(x):
    return x / (1.0 + jnp.exp(-x))


def _softplus(x):
    return jnp.maximum(x, 0.0) + jnp.log1p(jnp.exp(-jnp.abs(x)))


def _rope_chunk(t, cos, sin_signed, lower):
    rot = jnp.where(lower, pltpu.roll(t, 96, 1), pltpu.roll(t, 32, 1))
    return t * cos + rot * sin_signed


def _const_spec(shape):
    n = len(shape)
    return pl.BlockSpec(shape, lambda *_: (0,) * n, pipeline_mode=pl.Buffered(1))


def _ada_kernel(c_ref, w_ref, b_ref, o_ref):
    c = _silu(c_ref[...]).astype(BF16)
    o_ref[...] = _dot(c, w_ref[...].astype(BF16)) + b_ref[...]


def _ada(c_all, w_ada, b_ada):
    m, d = c_all.shape
    n = w_ada.shape[1]
    tn = 1024
    return pl.pallas_call(
        _ada_kernel,
        grid=(n // tn,),
        in_specs=[pl.BlockSpec((m, d), lambda j: (0, 0)),
                  pl.BlockSpec((d, tn), lambda j: (0, j)),
                  pl.BlockSpec((1, tn), lambda j: (0, j))],
        out_specs=pl.BlockSpec((m, tn), lambda j: (0, j)),
        out_shape=jax.ShapeDtypeStruct((m, n), F32),
        compiler_params=pltpu.CompilerParams(dimension_semantics=("arbitrary",), vmem_limit_bytes=VMEM_LIMIT),
        name="ada_mod",
    )(c_all, w_ada, b_ada)


def _inproj_prompt_kernel(x_ref, mod_ref, w_ref, wdt_ref, cos_ref, sin_ref, convw_ref, convb_ref, dtb_ref,
                          q_ref, k_ref, v_ref, z_ref, xbc_ref, dt_ref, dtT_ref, kwin_ref, vwin_ref, convst_ref,
                          buf_ref, *, tm, n_tiles):
    i = pl.program_id(1)
    x = x_ref[0]
    h = _rms(x) * (1.0 + mod_ref[0, 1:2, :]) + mod_ref[0, 0:1, :]
    hb = h.astype(BF16)
    cos = cos_ref[...]
    sin = sin_ref[...]
    lane = lax.broadcasted_iota(jnp.int32, (tm, LANES), 1)
    lower = (lane % HEAD_DIM) < (HEAD_DIM // 2)

    qq = _dot(hb, w_ref[:, 0:D_ATTN])
    kk = _dot(hb, w_ref[:, D_ATTN:2 * D_ATTN])
    for c in range(D_ATTN // LANES):
        sl = slice(c * LANES, (c + 1) * LANES)
        qc = _rope_chunk(qq[:, sl], cos, sin, lower)
        q_ref[0, :, sl] = (qc * (HEAD_DIM ** -0.5)).astype(BF16)
        kc = _rope_chunk(kk[:, sl], cos, sin, lower)
        k_ref[0, :, sl] = kc.astype(BF16)
        kwin_ref[0, :, sl] = kc
    vv = _dot(hb, w_ref[:, 2 * D_ATTN:3 * D_ATTN])
    v_ref[0] = vv.astype(BF16)
    vwin_ref[0] = vv
    z_ref[0] = _dot(hb, w_ref[:, 3 * D_ATTN:3 * D_ATTN + D_SSM]).astype(BF16)

    @pl.when(i == 0)
    def _():
        buf_ref[0:8, :] = jnp.zeros((8, D_CONV), F32)

    base = 3 * D_ATTN + D_SSM
    cw = 512
    for cc in range(D_CONV // cw):
        sl = slice(cc * cw, (cc + 1) * cw)
        buf_ref[8:8 + tm, sl] = _dot(hb, w_ref[:, base + cc * cw:base + (cc + 1) * cw])
        acc = jnp.broadcast_to(convb_ref[:, sl], (tm, cw))
        for w in range(CONV_WIDTH):
            acc = acc + convw_ref[w:w + 1, sl] * buf_ref[5 + w:5 + w + tm, sl]
        xbc_ref[0, :, sl] = _silu(acc).astype(BF16)

    @pl.when(i == n_tiles - 1)
    def _():
        convst_ref[0] = buf_ref[tm + 5:tm + 8, :]

    buf_ref[0:8, :] = buf_ref[tm:tm + 8, :]

    dt = _softplus(_dot(hb, wdt_ref[...]) + dtb_ref[...])
    dt_ref[0] = dt[:, :N_HEADS]
    dtT_ref[0] = dt.T[:N_HEADS, :]


def _inproj_prompt(x, mod, w_main, w_dt, cos, sin, conv_w, conv_b, dtb_pad, tm=512):
    b, s, d = x.shape
    n_tiles = s // tm
    win = min(MAX_WINDOW, s)
    first_win_tile = (s - win) // tm
    row = lambda bi, i: (bi, i, 0)
    winrow = lambda bi, i: (bi, jnp.maximum(i - first_win_tile, 0), 0)
    kern = functools.partial(_inproj_prompt_kernel, tm=tm, n_tiles=n_tiles)
    out_shape = [
        jax.ShapeDtypeStruct((b, s, D_ATTN), BF16),
        jax.ShapeDtypeStruct((b, s, D_ATTN), BF16),
        jax.ShapeDtypeStruct((b, s, D_ATTN), BF16),
        jax.ShapeDtypeStruct((b, s, D_SSM), BF16),
        jax.ShapeDtypeStruct((b, s, D_CONV), BF16),
        jax.ShapeDtypeStruct((b, s, N_HEADS), F32),
        jax.ShapeDtypeStruct((b, N_HEADS, s), F32),
        jax.ShapeDtypeStruct((b, win, D_ATTN), F32),
        jax.ShapeDtypeStruct((b, win, D_ATTN), F32),
        jax.ShapeDtypeStruct((b, CONV_WIDTH - 1, D_CONV), F32),
    ]
    out_specs = [
        pl.BlockSpec((1, tm, D_ATTN), row),
        pl.BlockSpec((1, tm, D_ATTN), row),
        pl.BlockSpec((1, tm, D_ATTN), row),
        pl.BlockSpec((1, tm, D_SSM), row),
        pl.BlockSpec((1, tm, D_CONV), row),
        pl.BlockSpec((1, tm, N_HEADS), row),
        pl.BlockSpec((1, N_HEADS, tm), lambda bi, i: (bi, 0, i)),
        pl.BlockSpec((1, tm, D_ATTN), winrow),
        pl.BlockSpec((1, tm, D_ATTN), winrow),
        pl.BlockSpec((1, CONV_WIDTH - 1, D_CONV), lambda bi, i: (bi, 0, 0)),
    ]
    in_specs = [
        pl.BlockSpec((1, tm, d), row),
        pl.BlockSpec((1, 6, d), lambda bi, i: (bi, 0, 0)),
        _const_spec(w_main.shape),
        _const_spec(w_dt.shape),
        pl.BlockSpec((tm, LANES), lambda bi, i: (i, 0)),
        pl.BlockSpec((tm, LANES), lambda bi, i: (i, 0)),
        _const_spec(conv_w.shape),
        _const_spec(conv_b.shape),
        _const_spec(dtb_pad.shape),
    ]
    return pl.pallas_call(
        kern, grid=(b, n_tiles), in_specs=in_specs, out_specs=out_specs, out_shape=out_shape,
        scratch_shapes=[pltpu.VMEM((tm + 8, D_CONV), F32)],
        compiler_params=pltpu.CompilerParams(dimension_semantics=("arbitrary", "arbitrary"),
                                             vmem_limit_bytes=VMEM_LIMIT),
        name="inproj_prompt",
    )(x, mod, w_main, w_dt, cos, sin, conv_w, conv_b, dtb_pad)


def _ssd_kernel(xbc_ref, z_ref, dt_ref, dtT_ref, alog_ref, alogT_ref, dskip_ref, gssm_ref,
                y_ref, hlast_ref, state_ref, ybuf_ref, *, n_chunks):
    c = pl.program_id(1)

    @pl.when(c == 0)
    def _():
        state_ref[...] = jnp.zeros_like(state_ref)

    a_row = -jnp.exp(alog_ref[...])
    a_col = -jnp.exp(alogT_ref[...])
    dt = dt_ref[0]
    dtT = dtT_ref[0]
    row = lax.broadcasted_iota(jnp.int32, (BLK, BLK), 0)
    col = lax.broadcasted_iota(jnp.int32, (BLK, BLK), 1)
    tril = row >= col
    tri_l = tril.astype(BF16)
    tri_u = (row <= col).astype(BF16)
    lo = col < HEAD_DIM

    cum = sum(_dot(tri_l, p) for p in _split3(dt * a_row))
    cumT = sum(_dot(p, tri_u) for p in _split3(dtT * a_col))
    ecum = jnp.exp(cum)
    dte = jnp.exp(cum[BLK - 1:BLK, :] - cum) * dt
    ecl = jnp.exp(cumT[:, BLK - 1:BLK])

    for g in range(SSM_GROUPS):
        bg = xbc_ref[0, :, D_SSM + g * D_STATE:D_SSM + (g + 1) * D_STATE]
        cg = xbc_ref[0, :, D_SSM + (SSM_GROUPS + g) * D_STATE:D_SSM + (SSM_GROUPS + g + 1) * D_STATE]
        cb = _dot_nt(cg, bg)
        for jj in range(2):
            j = 2 * g + jj
            h0, h1 = 2 * j, 2 * j + 1
            xp = xbc_ref[0, :, j * LANES:(j + 1) * LANES]
            y_pair = jnp.zeros((BLK, LANES), F32)
            for hm, h in ((lo, h0), (~lo, h1)):
                seg = cum[:, h:h + 1] - cumT[h:h + 1, :]
                dec = jnp.where(tril, jnp.exp(seg), 0.0)
                m = (cb * dec * dtT[h:h + 1, :]).astype(BF16)
                y_pair = y_pair + _dot(m, jnp.where(hm, xp, jnp.zeros_like(xp)))
            s_pair = state_ref[j]
            e_pair = jnp.where(lo, ecum[:, h0:h0 + 1], ecum[:, h1:h1 + 1])
            y_pair = y_pair + _dot_nt(cg, s_pair.astype(BF16)) * e_pair
            w_pair = jnp.where(lo, dte[:, h0:h0 + 1], dte[:, h1:h1 + 1])
            xf = xp.astype(F32)
            upd = _dot_tn((xf * w_pair).astype(BF16), bg)
            scale = jnp.where(row < HEAD_DIM, ecl[h0:h0 + 1, :], ecl[h1:h1 + 1, :])
            state_ref[j] = s_pair * scale + upd
            sl = slice(j * LANES, (j + 1) * LANES)
            ybuf_ref[:, sl] = y_pair + dskip_ref[:, sl] * xf

    gw = D_SSM // SSM_GROUPS
    for g in range(SSM_GROUPS):
        sl = slice(g * gw, (g + 1) * gw)
        y = ybuf_ref[:, sl] * _silu(z_ref[0, :, sl].astype(F32))
        y_ref[0, :, sl] = (_rms(y) * gssm_ref[:, sl]).astype(BF16)

    @pl.when(c == n_chunks - 1)
    def _():
        hlast_ref[0] = state_ref[...]


def _ssd_prompt(xbc, z, dt, dtT, a_log, a_logT, dskip_lane, g_ssm):
    b, s, _ = xbc.shape
    n_chunks = s // BLK
    row = lambda bi, c: (bi, c, 0)
    return pl.pallas_call(
        functools.partial(_ssd_kernel, n_chunks=n_chunks),
        grid=(b, n_chunks),
        in_specs=[
            pl.BlockSpec((1, BLK, D_CONV), row),
            pl.BlockSpec((1, BLK, D_SSM), row),
            pl.BlockSpec((1, BLK, N_HEADS), row),
            pl.BlockSpec((1, N_HEADS, BLK), lambda bi, c: (bi, 0, c)),
            pl.BlockSpec((1, N_HEADS), lambda bi, c: (0, 0)),
            pl.BlockSpec((N_HEADS, 1), lambda bi, c: (0, 0)),
            pl.BlockSpec((1, D_SSM), lambda bi, c: (0, 0)),
            pl.BlockSpec((1, D_SSM), lambda bi, c: (0, 0)),
        ],
        out_specs=[
            pl.BlockSpec((1, BLK, D_SSM), row),
            pl.BlockSpec((1, N_HEADS // 2, LANES, D_STATE), lambda bi, c: (bi, 0, 0, 0)),
        ],
        out_shape=[
            jax.ShapeDtypeStruct((b, s, D_SSM), BF16),
            jax.ShapeDtypeStruct((b, N_HEADS // 2, LANES, D_STATE), F32),
        ],
        scratch_shapes=[pltpu.VMEM((N_HEADS // 2, LANES, D_STATE), F32), pltpu.VMEM((BLK, D_SSM), F32)],
        compiler_params=pltpu.CompilerParams(dimension_semantics=("arbitrary", "arbitrary"),
                                             vmem_limit_bytes=VMEM_LIMIT),
        name="ssd_prompt",
    )(xbc, z, dt, dtT, a_log, a_logT, dskip_lane, g_ssm)


SUPER = MAX_WINDOW
GROUP = 8


def _attn_kernel(q_ref, kp_ref, kc_ref, vp_ref, vc_ref, o_ref, qf, kf, vf, acc_s, m_s, l_s):
    t = pl.program_id(2)
    qf[...] = q_ref[0].astype(F32)
    kf[SUPER:, :] = kc_ref[0].astype(F32)
    vf[SUPER:, :] = vc_ref[0].astype(F32)

    @pl.when(t == 0)
    def _():
        kf[0:SUPER, :] = jnp.zeros((SUPER, LANES), F32)
        vf[0:SUPER, :] = jnp.zeros((SUPER, LANES), F32)

    @pl.when(t > 0)
    def _():
        kf[0:SUPER, :] = kp_ref[0].astype(F32)
        vf[0:SUPER, :] = vp_ref[0].astype(F32)

    qi = lax.broadcasted_iota(jnp.int32, (BLK, 2 * BLK), 0)
    kj = lax.broadcasted_iota(jnp.int32, (BLK, 2 * BLK), 1)
    band = (kj >= qi) & (kj <= qi + BLK)
    in_cur = kj >= BLK
    lo = lax.broadcasted_iota(jnp.int32, (BLK, LANES), 1) < HEAD_DIM
    lo_k = lax.broadcasted_iota(jnp.int32, (2 * BLK, LANES), 1) < HEAD_DIM

    for ci, (_, dil) in enumerate(DILATED_CONFIGS):
        nblk = SUPER // (BLK * dil)

        def group(g, carry, ci=ci, dil=dil, nblk=nblk):
            blocks = []
            for u in range(GROUP):
                if nblk == GROUP:
                    r, n = g, u
                elif nblk > GROUP:
                    r, n = g // (nblk // GROUP), (g % (nblk // GROUP)) * GROUP + u
                else:
                    r, n = g * (GROUP // nblk) + u // nblk, u % nblk
                qs = r + dil * BLK * n
                ks = SUPER + r + dil * BLK * (n - 1)
                stride = None if dil == 1 else dil
                qb = qf[pl.ds(qs, BLK, stride=stride), :].astype(BF16)
                kb = kf[pl.ds(ks, 2 * BLK, stride=stride), :].astype(BF16)
                vb = vf[pl.ds(ks, 2 * BLK, stride=stride), :].astype(BF16)
                mask = band & (in_cur | (t > 0) | (n > 0))
                blocks.append((qs, qb, kb, vb, mask))
            stats = []
            for qs, qb, kb, vb, mask in blocks:
                for hm in (lo, ~lo):
                    s = jnp.where(mask, _dot_nt(jnp.where(hm, qb, jnp.zeros_like(qb)), kb), NEG_INF)
                    stats.append((s, jnp.max(s, axis=1, keepdims=True)))
            probs = []
            for s, m in stats:
                p = jnp.exp(s - m)
                probs.append((p.astype(BF16), jnp.sum(p, axis=1, keepdims=True)))
            for bi, (qs, qb, kb, vb, mask) in enumerate(blocks):
                (p0, l0), (p1, l1) = probs[2 * bi], probs[2 * bi + 1]
                m0, m1 = stats[2 * bi][1], stats[2 * bi + 1][1]
                zero = jnp.zeros_like(vb)
                acc = _dot(p0, jnp.where(lo_k, vb, zero)) + _dot(p1, jnp.where(lo_k, zero, vb))
                stride = None if dil == 1 else dil
                rows = pl.ds(qs, BLK, stride=stride)
                acc_s[ci, rows, :] = acc
                m_s[ci, rows, :] = jnp.where(lo, m0, m1)
                l_s[ci, rows, :] = jnp.where(lo, l0, l1)
            return carry

        lax.fori_loop(0, (SUPER // BLK) // GROUP, group, 0)

    n_cfg = len(DILATED_CONFIGS)
    rc = 256
    for c0 in range(0, SUPER, rc):
        rows = slice(c0, c0 + rc)
        ms = [m_s[ci, rows, :] for ci in range(n_cfg)]
        mx = functools.reduce(jnp.maximum, ms)
        num = jnp.zeros((rc, LANES), F32)
        den = jnp.zeros((rc, LANES), F32)
        for ci in range(n_cfg):
            a = jnp.exp(ms[ci] - mx)
            num = num + a * acc_s[ci, rows, :]
            den = den + a * l_s[ci, rows, :]
        o_ref[0, rows, :] = (num / den).astype(o_ref.dtype)


def _attn_prompt(q, k, v):
    b, s, d = q.shape
    n_cfg = len(DILATED_CONFIGS)
    cur = lambda bi, j, t: (bi, t, j)
    prev = lambda bi, j, t: (bi, jnp.maximum(t - 1, 0), j)
    blk = (1, SUPER, LANES)
    return pl.pallas_call(
        _attn_kernel,
        grid=(b, d // LANES, s // SUPER),
        in_specs=[pl.BlockSpec(blk, cur), pl.BlockSpec(blk, prev), pl.BlockSpec(blk, cur),
                  pl.BlockSpec(blk, prev), pl.BlockSpec(blk, cur)],
        out_specs=pl.BlockSpec(blk, cur),
        out_shape=jax.ShapeDtypeStruct((b, s, d), BF16),
        scratch_shapes=[pltpu.VMEM((SUPER, LANES), F32), pltpu.VMEM((2 * SUPER, LANES), F32),
                        pltpu.VMEM((2 * SUPER, LANES), F32), pltpu.VMEM((n_cfg, SUPER, LANES), F32),
                        pltpu.VMEM((n_cfg, SUPER, LANES), F32), pltpu.VMEM((n_cfg, SUPER, LANES), F32)],
        compiler_params=pltpu.CompilerParams(dimension_semantics=("arbitrary", "arbitrary", "arbitrary"),
                                             vmem_limit_bytes=VMEM_LIMIT),
        name="attn_prompt",
    )(q, k, k, v, v)


def _tail_kernel(x_ref, mod_ref, attn_ref, yssm_ref, gattn_ref, gfin_ref, wout_ref, wup_ref, wdown_ref, y_ref,
                 *, sample, tm):
    if sample:
        mod = lambda k: mod_ref[k]
        attn = attn_ref[...].T
    else:
        mod = lambda k: mod_ref[0, k:k + 1, :]
        attn = attn_ref[0].astype(F32)
    x = x_ref[...].reshape(tm, D_MODEL)
    attn_n = (_rms(attn) * gattn_ref[...]).astype(BF16)
    mix = _dot(attn_n, wout_ref[0:D_ATTN, :]) + _dot(yssm_ref[...].reshape(tm, D_SSM), wout_ref[D_ATTN:, :])
    x1 = x + mod(2) * mix
    h2 = (_rms(x1) * (1.0 + mod(4)) + mod(3)).astype(BF16)
    d_ff = wup_ref.shape[1]
    cw = 1024
    acc = jnp.zeros((tm, D_MODEL), F32)
    for cc in range(d_ff // cw):
        u = jnp.maximum(_dot(h2, wup_ref[:, cc * cw:(cc + 1) * cw]), 0.0)
        acc = acc + _dot((u * u).astype(BF16), wdown_ref[cc * cw:(cc + 1) * cw, :])
    x2 = x1 + mod(5) * acc
    y_ref[...] = (_rms(x2) * gfin_ref[...]).reshape(y_ref.shape)


def _tail_prompt(x, mod, attn, y_ssm, g_attn, g_final, w_out, w_up, w_down, tm=512):
    b, s, d = x.shape
    row = lambda bi, i: (bi, i, 0)
    in_specs = [pl.BlockSpec((1, tm, d), row), pl.BlockSpec((1, 6, d), lambda bi, i: (bi, 0, 0)),
                pl.BlockSpec((1, tm, D_ATTN), row), pl.BlockSpec((1, tm, D_SSM), row),
                _const_spec(g_attn.shape), _const_spec(g_final.shape),
                _const_spec(w_out.shape), _const_spec(w_up.shape), _const_spec(w_down.shape)]
    return pl.pallas_call(
        functools.partial(_tail_kernel, sample=False, tm=tm),
        grid=(b, s // tm), in_specs=in_specs,
        out_specs=pl.BlockSpec((1, tm, d), row),
        out_shape=jax.ShapeDtypeStruct((b, s, d), F32),
        compiler_params=pltpu.CompilerParams(dimension_semantics=("arbitrary", "arbitrary"),
                                             vmem_limit_bytes=VMEM_LIMIT),
        name="tail_prompt",
    )(x, mod, attn, y_ssm, g_attn, g_final, w_out, w_up, w_down)


def _tail_sample(x, mod_t, attn, y_ssm, g_attn, g_final, w_out, w_up, w_down):
    m, d = x.shape
    full = lambda shape: pl.BlockSpec(shape, lambda i: (0,) * len(shape))
    in_specs = [full((m, d)), full(mod_t.shape), full((D_ATTN, m)), full((m, D_SSM)),
                full(g_attn.shape), full(g_final.shape), full(w_out.shape), full(w_up.shape), full(w_down.shape)]
    return pl.pallas_call(
        functools.partial(_tail_kernel, sample=True, tm=m),
        grid=(1,), in_specs=in_specs,
        out_specs=full((m, d)),
        out_shape=jax.ShapeDtypeStruct((m, d), F32),
        compiler_params=pltpu.CompilerParams(dimension_semantics=("arbitrary",), vmem_limit_bytes=VMEM_LIMIT),
        name="tail_sample",
    )(x, mod_t, attn, y_ssm, g_attn, g_final, w_out, w_up, w_down)


def _inproj_sample_kernel(x_ref, mod_ref, w_ref, wdt_ref, cos_ref, sin_ref, convst_ref, convw_ref, convb_ref,
                          dtb_ref, qT_ref, kT_ref, vT_ref, z_ref, xbc_ref, xT_ref, dtT_ref, convout_ref):
    m = x_ref.shape[0]
    h = _rms(x_ref[...]) * (1.0 + mod_ref[1]) + mod_ref[0]
    hb = h.astype(BF16)
    cos = cos_ref[...]
    sin = sin_ref[...]
    lane = lax.broadcasted_iota(jnp.int32, (m, LANES), 1)
    lower = (lane % HEAD_DIM) < (HEAD_DIM // 2)
    for c in range(D_ATTN // LANES):
        sl = slice(c * LANES, (c + 1) * LANES)
        qc = _rope_chunk(_dot(hb, w_ref[:, c * LANES:(c + 1) * LANES]), cos, sin, lower)
        qT_ref[sl, :] = (qc * (HEAD_DIM ** -0.5)).T
        kc = _rope_chunk(_dot(hb, w_ref[:, D_ATTN + c * LANES:D_ATTN + (c + 1) * LANES]), cos, sin, lower)
        kT_ref[sl, :] = kc.T
        vT_ref[sl, :] = _dot(hb, w_ref[:, 2 * D_ATTN + c * LANES:2 * D_ATTN + (c + 1) * LANES]).T
    z_ref[...] = _dot(hb, w_ref[:, 3 * D_ATTN:3 * D_ATTN + D_SSM])
    base = 3 * D_ATTN + D_SSM
    xbc = _dot(hb, w_ref[:, base:base + D_CONV])
    acc = convb_ref[...] + convw_ref[CONV_WIDTH - 1:CONV_WIDTH, :] * xbc
    for w in range(CONV_WIDTH - 1):
        acc = acc + convw_ref[w:w + 1, :] * convst_ref[w]
    xc = _silu(acc)
    xbc_ref[...] = xc
    xT_ref[...] = xc[:, :D_SSM].T
    for w in range(CONV_WIDTH - 2):
        convout_ref[w] = convst_ref[w + 1]
    convout_ref[CONV_WIDTH - 2] = xbc
    dt = _softplus(_dot(hb, wdt_ref[...]) + dtb_ref[...])
    dtT_ref[...] = dt.T[:N_HEADS, :]


def _inproj_sample(x, mod_t, w_main, w_dt, cos, sin, conv_state, conv_w, conv_b, dtb_pad):
    m, d = x.shape
    full = lambda shape: pl.BlockSpec(shape, lambda i: (0,) * len(shape))
    ins = (x, mod_t, w_main, w_dt, cos, sin, conv_state, conv_w, conv_b, dtb_pad)
    out_shape = [
        jax.ShapeDtypeStruct((D_ATTN, m), F32), jax.ShapeDtypeStruct((D_ATTN, m), F32),
        jax.ShapeDtypeStruct((D_ATTN, m), F32), jax.ShapeDtypeStruct((m, D_SSM), F32),
        jax.ShapeDtypeStruct((m, D_CONV), F32), jax.ShapeDtypeStruct((D_SSM, m), F32),
        jax.ShapeDtypeStruct((N_HEADS, m), F32), jax.ShapeDtypeStruct((CONV_WIDTH - 1, m, D_CONV), F32),
    ]
    return pl.pallas_call(
        _inproj_sample_kernel, grid=(1,),
        in_specs=[full(a.shape) for a in ins],
        out_specs=[full(o.shape) for o in out_shape],
        out_shape=out_shape,
        compiler_params=pltpu.CompilerParams(dimension_semantics=("arbitrary",), vmem_limit_bytes=VMEM_LIMIT),
        name="inproj_sample",
    )(*ins)


def _attn_sample_kernel(qT_ref, knT_ref, vnT_ref, kc_ref, vc_ref, oT_ref, ko_ref, vo_ref, *, hpb, win, nb):
    b = pl.program_id(1)
    sel = lax.broadcasted_iota(jnp.int32, (HEAD_DIM, nb), 1) == b
    pick = lambda ref, rs: jnp.sum(jnp.where(sel, ref[rs, :], 0.0), axis=1, keepdims=True)
    nt = win // LANES
    lane = lax.broadcasted_iota(jnp.int32, (HEAD_DIM, LANES), 1)
    keep = lane < LANES - 1
    lane1 = lax.broadcasted_iota(jnp.int32, (1, LANES), 1)
    n_cfg = float(len(DILATED_CONFIGS))

    @pl.when(b == 0)
    def _():
        oT_ref[...] = jnp.zeros_like(oT_ref)

    def mult(c):
        w = lane1 + c * LANES
        tot = jnp.zeros((1, LANES), F32)
        for wd, dil in DILATED_CONFIGS:
            tot = tot + ((w >= win - wd) & ((win - w) % dil == 0)).astype(F32)
        return tot

    mults = [mult(c) for c in range(nt)]

    for h in range(hpb):
        rs = slice(h * HEAD_DIM, (h + 1) * HEAD_DIM)
        qc = pick(qT_ref, rs)
        knc = pick(knT_ref, rs)
        vnc = pick(vnT_ref, rs)
        s_self = jnp.sum(qc * knc, axis=0, keepdims=True)
        s_tiles = [None] * nt
        nxt = jnp.broadcast_to(knc, (HEAD_DIM, LANES))
        for c in reversed(range(nt)):
            tl = slice(c * LANES, (c + 1) * LANES)
            kt = kc_ref[0, h, :, tl]
            s_tiles[c] = jnp.where(mults[c] > 0, jnp.sum(kt * qc, axis=0, keepdims=True), NEG_INF)
            r = pltpu.roll(kt, LANES - 1, 1)
            ko_ref[0, h, :, tl] = jnp.where(keep, r, nxt)
            nxt = r
        m = jnp.maximum(jnp.max(functools.reduce(jnp.maximum, s_tiles), axis=1, keepdims=True), s_self)
        p_self = n_cfg * jnp.exp(s_self - m)
        p_tiles = [mults[c] * jnp.exp(s_tiles[c] - m) for c in range(nt)]
        l = jnp.sum(functools.reduce(lambda a, b_: a + b_, p_tiles), axis=1, keepdims=True) + p_self
        acc = jnp.zeros((HEAD_DIM, LANES), F32)
        nxt = jnp.broadcast_to(vnc, (HEAD_DIM, LANES))
        for c in reversed(range(nt)):
            tl = slice(c * LANES, (c + 1) * LANES)
            vt = vc_ref[0, h, :, tl]
            acc = acc + vt * p_tiles[c]
            r = pltpu.roll(vt, LANES - 1, 1)
            vo_ref[0, h, :, tl] = jnp.where(keep, r, nxt)
            nxt = r
        o = jnp.sum(acc, axis=1, keepdims=True) + p_self * vnc
        oT_ref[rs, :] = jnp.where(sel, o / l, oT_ref[rs, :])


def _attn_sample(qT, knT, vnT, cache_kT, cache_vT, hpb=8):
    nb, nh, hd, win = cache_kT.shape
    colspec = pl.BlockSpec((hpb * hd, nb), lambda g, b: (g, 0))
    cspec = pl.BlockSpec((1, hpb, hd, win), lambda g, b: (b, g, 0, 0))
    return pl.pallas_call(
        functools.partial(_attn_sample_kernel, hpb=hpb, win=win, nb=nb),
        grid=(nh // hpb, nb),
        in_specs=[colspec, colspec, colspec, cspec, cspec],
        out_specs=[colspec, cspec, cspec],
        out_shape=[jax.ShapeDtypeStruct((nh * hd, nb), F32),
                   jax.ShapeDtypeStruct(cache_kT.shape, F32), jax.ShapeDtypeStruct(cache_vT.shape, F32)],
        compiler_params=pltpu.CompilerParams(dimension_semantics=("arbitrary", "arbitrary"),
                                             vmem_limit_bytes=VMEM_LIMIT),
        name="attn_sample_shift",
    )(qT, knT, vnT, cache_kT, cache_vT)


def _ssm_sample_kernel(st_ref, xT_ref, b_ref, c_ref, dtT_ref, alogT_ref, new_ref, yT_ref, *, nb):
    h = pl.program_id(0)
    a_col = -jnp.exp(alogT_ref[...])
    dtT = dtT_ref[...]
    decT = jnp.exp(dtT * a_col)
    hsel = lax.broadcasted_iota(jnp.int32, (N_HEADS, nb), 0) == h
    dt_h = jnp.sum(jnp.where(hsel, dtT, 0.0), axis=0, keepdims=True)
    dec_h = jnp.sum(jnp.where(hsel, decT, 0.0), axis=0, keepdims=True)
    xdt = xT_ref[...] * dt_h
    lane = lax.broadcasted_iota(jnp.int32, (HEAD_DIM, nb), 1)
    y_acc = jnp.zeros((HEAD_DIM, nb), F32)
    rows_per_batch = 8
    for b0 in range(0, nb, rows_per_batch):
        prods = []
        for bi in range(b0, b0 + rows_per_batch):
            outer = xdt[:, bi:bi + 1] * b_ref[bi:bi + 1, :]
            new = st_ref[0, bi, 0] * dec_h[:, bi:bi + 1] + outer
            new_ref[0, bi, 0] = new
            prods.append(new * c_ref[bi:bi + 1, :])
        cols = [jnp.sum(p, axis=1, keepdims=True) for p in prods]
        for k, ycol in enumerate(cols):
            y_acc = jnp.where(lane == b0 + k, ycol, y_acc)
    yT_ref[...] = y_acc


def _ssm_sample(state, xT, xbc, dtT, a_logT):
    _, nb, nh, hp, n = state.shape
    hg = nh // SSM_GROUPS
    b_blk0 = D_SSM // D_STATE
    return pl.pallas_call(
        functools.partial(_ssm_sample_kernel, nb=nb),
        grid=(nh,),
        in_specs=[pl.BlockSpec((1, nb, 1, hp, n), lambda h: (0, 0, h, 0, 0)),
                  pl.BlockSpec((hp, nb), lambda h: (h, 0)),
                  pl.BlockSpec((nb, D_STATE), lambda h: (0, b_blk0 + h // hg)),
                  pl.BlockSpec((nb, D_STATE), lambda h: (0, b_blk0 + SSM_GROUPS + h // hg)),
                  pl.BlockSpec((nh, nb), lambda h: (0, 0)),
                  pl.BlockSpec((nh, 1), lambda h: (0, 0))],
        out_specs=[pl.BlockSpec((1, nb, 1, hp, n), lambda h: (0, 0, h, 0, 0)),
                   pl.BlockSpec((hp, nb), lambda h: (h, 0))],
        out_shape=[jax.ShapeDtypeStruct(state.shape, F32), jax.ShapeDtypeStruct((nh * hp, nb), F32)],
        compiler_params=pltpu.CompilerParams(dimension_semantics=("arbitrary",), vmem_limit_bytes=VMEM_LIMIT),
        name="ssm_sample",
    )(state, xT, xbc, xbc, dtT, a_logT)


def _ssm_gate_kernel(yT_ref, xs_ref, z_ref, dskip_ref, gssm_ref, o_ref):
    y = yT_ref[...].T + dskip_ref[...] * xs_ref[...]
    y = y * _silu(z_ref[...])
    gw = D_SSM // SSM_GROUPS
    for g in range(SSM_GROUPS):
        sl = slice(g * gw, (g + 1) * gw)
        o_ref[:, sl] = (_rms(y[:, sl]) * gssm_ref[:, sl]).astype(BF16)


def _ssm_gate(yT, xbc, z, dskip_lane, g_ssm):
    m = z.shape[0]
    full = lambda shape: pl.BlockSpec(shape, lambda i: (0,) * len(shape))
    return pl.pallas_call(
        _ssm_gate_kernel, grid=(1,),
        in_specs=[full(yT.shape), pl.BlockSpec((m, D_SSM), lambda i: (0, 0)), full(z.shape),
                  full(dskip_lane.shape), full(g_ssm.shape)],
        out_specs=full((m, D_SSM)),
        out_shape=jax.ShapeDtypeStruct((m, D_SSM), BF16),
        compiler_params=pltpu.CompilerParams(dimension_semantics=("arbitrary",), vmem_limit_bytes=VMEM_LIMIT),
        name="ssm_gate_sample",
    )(yT, xbc, z, dskip_lane, g_ssm)


def _rope_tables(pos):
    half = HEAD_DIM // 2
    inv_freq = ROPE_THETA ** (-jnp.arange(half, dtype=F32) / half)
    ang = pos.astype(F32)[:, None] * inv_freq[None, :]
    cos = jnp.tile(jnp.cos(ang), (1, LANES // half))
    sin = jnp.tile(jnp.sin(ang), (1, LANES // half))
    sign = jnp.where((jnp.arange(LANES) % HEAD_DIM) < half, -1.0, 1.0).astype(F32)
    return cos, sin * sign[None, :]


def kernel(x_prompt, x_sample, c_prompt, c_sample, cache_k_win, cache_v_win, state_conv, state_ssm, w_ada, b_ada,
           w_in, conv_w, conv_b, dt_bias, a_log, d_skip, g_attn, g_ssm, w_out, w_up, w_down, g_final):
    depth = w_ada.shape[0]
    assert depth == 1, "single-layer trunk"
    b, s, d = x_prompt.shape
    nb, t_new, _ = x_sample.shape
    assert t_new == 1
    win = cache_k_win.shape[2]
    assert win == MAX_WINDOW and s % MAX_WINDOW == 0
    past_len = PAST_LEN
    layer = 0

    w_in_l = w_in[layer]
    w_main = w_in_l[:, :D_MAIN].astype(BF16)
    w_dt = jnp.pad(w_in_l[:, D_MAIN:], ((0, 0), (0, LANES - N_HEADS))).astype(BF16)
    dtb_pad = jnp.pad(dt_bias[layer].astype(F32), (0, LANES - N_HEADS))[None, :]
    w_out_b = w_out[layer].astype(BF16)
    w_up_b = w_up[layer].astype(BF16)
    w_down_b = w_down[layer].astype(BF16)
    conv_w_l = conv_w[layer]
    conv_b_l = conv_b[layer][None, :]
    a_log_row = a_log[layer][None, :]
    a_log_col = a_log[layer][:, None]
    dskip_lane = jnp.repeat(d_skip[layer].astype(F32), HEAD_DIM)[None, :]
    g_attn_l = g_attn[layer][None, :]
    g_ssm_l = g_ssm[layer][None, :]
    g_final_l = g_final[None, :]

    pad_rows = (-(nb + b)) % 8
    c_all = jnp.concatenate([c_sample, c_prompt, jnp.zeros((pad_rows, d), F32)], axis=0)
    mod = _ada(c_all, w_ada[layer], b_ada[layer][None, :])
    mod_s = mod[:nb].reshape(nb, 6, d).transpose(1, 0, 2)
    mod_p = mod[nb:nb + b].reshape(b, 6, d)

    cos_p, sin_p = _rope_tables(jnp.arange(s, dtype=jnp.int32))
    q, k, v, z, xbc, dt, dtT, k_win, v_win, conv_p = _inproj_prompt(
        x_prompt, mod_p, w_main, w_dt, cos_p, sin_p, conv_w_l, conv_b_l, dtb_pad)
    y_ssm, h_last = _ssd_prompt(xbc, z, dt, dtT, a_log_row, a_log_col, dskip_lane, g_ssm_l)
    attn = _attn_prompt(q, k, v)
    y_prompt = _tail_prompt(x_prompt, mod_p, attn, y_ssm, g_attn_l, g_final_l, w_out_b, w_up_b, w_down_b)

    cos_s, sin_s = _rope_tables(past_len + jnp.arange(1, dtype=jnp.int32))
    qT_s, kT_s, vT_s, z_s, xbc_s, xT_s, dtT_s, conv_s = _inproj_sample(
        x_sample.reshape(nb, d), mod_s, w_main, w_dt, cos_s, sin_s, state_conv[layer].transpose(1, 0, 2),
        conv_w_l, conv_b_l, dtb_pad)
    attnT_s, k_winT_s, v_winT_s = _attn_sample(qT_s, kT_s, vT_s,
                                               cache_k_win[layer].transpose(0, 2, 3, 1),
                                               cache_v_win[layer].transpose(0, 2, 3, 1))
    ssm_s, yT_s = _ssm_sample(state_ssm[layer][None], xT_s, xbc_s, dtT_s, a_log_col)
    y_ssm_s = _ssm_gate(yT_s, xbc_s, z_s, dskip_lane, g_ssm_l)
    y_sample = _tail_sample(x_sample.reshape(nb, d), mod_s, attnT_s, y_ssm_s,
                            g_attn_l, g_final_l, w_out_b, w_up_b, w_down_b)

    hd = (N_HEADS, HEAD_DIM)
    return (y_prompt, y_sample.reshape(nb, 1, d),
            k_win.reshape(1, b, -1, *hd), v_win.reshape(1, b, -1, *hd),
            conv_p[None], h_last.reshape(1, b, N_HEADS, HEAD_DIM, D_STATE),
            k_winT_s.transpose(0, 3, 1, 2)[None], v_winT_s.transpose(0, 3, 1, 2)[None],
            conv_s.transpose(1, 0, 2)[None], ssm_s)
```

```python
import functools
import math

import jax
import jax.numpy as jnp
from jax import lax
from jax.experimental import pallas as pl
from jax.experimental.pallas import tpu as pltpu

F32 = jnp.float32
BF16 = jnp.bfloat16

HEAD_DIM = 64
N_HEADS = 16
D_MODEL = 1024
D_ATTN = 1024
D_SSM = 1024
SSM_GROUPS = 4
D_STATE = 128
CONV_WIDTH = 4
D_CONV = D_SSM + 2 * SSM_GROUPS * D_STATE
D_MAIN = 3 * D_ATTN + D_SSM + D_CONV
DILATED_CONFIGS = ((128, 1), (512, 4), (2048, 16))
MAX_WINDOW = 2048
BLK = 128
ROPE_THETA = 10000.0
NEG_INF = -1e30
EPS = 1e-6
LANES = 128
VMEM_LIMIT = 56 * 1024 * 1024
PAST_LEN = 8192


def _dot(a, b):
    return jnp.dot(a, b, preferred_element_type=F32)


def _dot_nt(a, b):
    return lax.dot_general(a, b, (((1,), (1,)), ((), ())), preferred_element_type=F32)


def _dot_tn(a, b):
    return lax.dot_general(a, b, (((0,), (0,)), ((), ())), preferred_element_type=F32)


def _split3(x):
    hi = x.astype(BF16)
    r = x - hi.astype(F32)
    mid = r.astype(BF16)
    lo = (r - mid.astype(F32)).astype(BF16)
    return hi, mid, lo


def _rms(x):
    return x * lax.rsqrt(jnp.mean(x * x, axis=-1, keepdims=True) + EPS)


def _silu(x):
    return x / (1.0 + jnp.exp(-x))


def _softplus(x):
    return jnp.maximum(x, 0.0) + jnp.log1p(jnp.exp(-jnp.abs(x)))


def _rope_chunk(t, cos, sin_signed, lower):
    rot = jnp.where(lower, pltpu.roll(t, 96, 1), pltpu.roll(t, 32, 1))
    return t * cos + rot * sin_signed


def _const_spec(shape):
    n = len(shape)
    return pl.BlockSpec(shape, lambda *_: (0,) * n, pipeline_mode=pl.Buffered(1))


def _ada_kernel(c_ref, w_ref, b_ref, o_ref):
    c = _silu(c_ref[...]).astype(BF16)
    o_ref[...] = _dot(c, w_ref[...].astype(BF16)) + b_ref[...]


def _ada(c_all, w_ada, b_ada):
    m, d = c_all.shape
    n = w_ada.shape[1]
    tn = 1024
    return pl.pallas_call(
        _ada_kernel,
        grid=(n // tn,),
        in_specs=[pl.BlockSpec((m, d), lambda j: (0, 0)),
                  pl.BlockSpec((d, tn), lambda j: (0, j)),
                  pl.BlockSpec((1, tn), lambda j: (0, j))],
        out_specs=pl.BlockSpec((m, tn), lambda j: (0, j)),
        out_shape=jax.ShapeDtypeStruct((m, n), F32),
        compiler_params=pltpu.CompilerParams(dimension_semantics=("arbitrary",), vmem_limit_bytes=VMEM_LIMIT),
        name="ada_mod",
    )(c_all, w_ada, b_ada)


def _inproj_prompt_kernel(x_ref, mod_ref, w_ref, wdt_ref, cos_ref, sin_ref, convw_ref, convb_ref, dtb_ref,
                          q_ref, k_ref, v_ref, z_ref, xbc_ref, dt_ref, dtT_ref, kwin_ref, vwin_ref, convst_ref,
                          buf_ref, *, tm, n_tiles):
    i = pl.program_id(1)
    x = x_ref[0]
    h = _rms(x) * (1.0 + mod_ref[0, 1:2, :]) + mod_ref[0, 0:1, :]
    hb = h.astype(BF16)
    cos = cos_ref[...]
    sin = sin_ref[...]
    lane = lax.broadcasted_iota(jnp.int32, (tm, LANES), 1)
    lower = (lane % HEAD_DIM) < (HEAD_DIM // 2)

    qq = _dot(hb, w_ref[:, 0:D_ATTN])
    kk = _dot(hb, w_ref[:, D_ATTN:2 * D_ATTN])
    for c in range(D_ATTN // LANES):
        sl = slice(c * LANES, (c + 1) * LANES)
        qc = _rope_chunk(qq[:, sl], cos, sin, lower)
        q_ref[0, :, sl] = (qc * (HEAD_DIM ** -0.5)).astype(BF16)
        kc = _rope_chunk(kk[:, sl], cos, sin, lower)
        k_ref[0, :, sl] = kc.astype(BF16)
        kwin_ref[0, :, sl] = kc
    vv = _dot(hb, w_ref[:, 2 * D_ATTN:3 * D_ATTN])
    v_ref[0] = vv.astype(BF16)
    vwin_ref[0] = vv
    z_ref[0] = _dot(hb, w_ref[:, 3 * D_ATTN:3 * D_ATTN + D_SSM]).astype(BF16)

    @pl.when(i == 0)
    def _():
        buf_ref[0:8, :] = jnp.zeros((8, D_CONV), F32)

    base = 3 * D_ATTN + D_SSM
    cw = 512
    for cc in range(D_CONV // cw):
        sl = slice(cc * cw, (cc + 1) * cw)
        buf_ref[8:8 + tm, sl] = _dot(hb, w_ref[:, base + cc * cw:base + (cc + 1) * cw])
        acc = jnp.broadcast_to(convb_ref[:, sl], (tm, cw))
        for w in range(CONV_WIDTH):
            acc = acc + convw_ref[w:w + 1, sl] * buf_ref[5 + w:5 + w + tm, sl]
        xbc_ref[0, :, sl] = _silu(acc).astype(BF16)

    @pl.when(i == n_tiles - 1)
    def _():
        convst_ref[0] = buf_ref[tm + 5:tm + 8, :]

    buf_ref[0:8, :] = buf_ref[tm:tm + 8, :]

    dt = _softplus(_dot(hb, wdt_ref[...]) + dtb_ref[...])
    dt_ref[0] = dt[:, :N_HEADS]
    dtT_ref[0] = dt.T[:N_HEADS, :]


def _inproj_prompt(x, mod, w_main, w_dt, cos, sin, conv_w, conv_b, dtb_pad, tm=512):
    b, s, d = x.shape
    n_tiles = s // tm
    win = min(MAX_WINDOW, s)
    first_win_tile = (s - win) // tm
    row = lambda bi, i: (bi, i, 0)
    winrow = lambda bi, i: (bi, jnp.maximum(i - first_win_tile, 0), 0)
    kern = functools.partial(_inproj_prompt_kernel, tm=tm, n_tiles=n_tiles)
    out_shape = [
        jax.ShapeDtypeStruct((b, s, D_ATTN), BF16),
        jax.ShapeDtypeStruct((b, s, D_ATTN), BF16),
        jax.ShapeDtypeStruct((b, s, D_ATTN), BF16),
        jax.ShapeDtypeStruct((b, s, D_SSM), BF16),
        jax.ShapeDtypeStruct((b, s, D_CONV), BF16),
        jax.ShapeDtypeStruct((b, s, N_HEADS), F32),
        jax.ShapeDtypeStruct((b, N_HEADS, s), F32),
        jax.ShapeDtypeStruct((b, win, D_ATTN), F32),
        jax.ShapeDtypeStruct((b, win, D_ATTN), F32),
        jax.ShapeDtypeStruct((b, CONV_WIDTH - 1, D_CONV), F32),
    ]
    out_specs = [
        pl.BlockSpec((1, tm, D_ATTN), row),
        pl.BlockSpec((1, tm, D_ATTN), row),
        pl.BlockSpec((1, tm, D_ATTN), row),
        pl.BlockSpec((1, tm, D_SSM), row),
        pl.BlockSpec((1, tm, D_CONV), row),
        pl.BlockSpec((1, tm, N_HEADS), row),
        pl.BlockSpec((1, N_HEADS, tm), lambda bi, i: (bi, 0, i)),
        pl.BlockSpec((1, tm, D_ATTN), winrow),
        pl.BlockSpec((1, tm, D_ATTN), winrow),
        pl.BlockSpec((1, CONV_WIDTH - 1, D_CONV), lambda bi, i: (bi, 0, 0)),
    ]
    in_specs = [
        pl.BlockSpec((1, tm, d), row),
        pl.BlockSpec((1, 6, d), lambda bi, i: (bi, 0, 0)),
        _const_spec(w_main.shape),
        _const_spec(w_dt.shape),
        pl.BlockSpec((tm, LANES), lambda bi, i: (i, 0)),
        pl.BlockSpec((tm, LANES), lambda bi, i: (i, 0)),
        _const_spec(conv_w.shape),
        _const_spec(conv_b.shape),
        _const_spec(dtb_pad.shape),
    ]
    return pl.pallas_call(
        kern, grid=(b, n_tiles), in_specs=in_specs, out_specs=out_specs, out_shape=out_shape,
        scratch_shapes=[pltpu.VMEM((tm + 8, D_CONV), F32)],
        compiler_params=pltpu.CompilerParams(dimension_semantics=("arbitrary", "arbitrary"),
                                             vmem_limit_bytes=VMEM_LIMIT),
        name="inproj_prompt",
    )(x, mod, w_main, w_dt, cos, sin, conv_w, conv_b, dtb_pad)


def _ssd_kernel(xbc_ref, z_ref, dt_ref, dtT_ref, alog_ref, alogT_ref, dskip_ref, gssm_ref,
                y_ref, hlast_ref, state_ref, ybuf_ref, *, n_chunks):
    c = pl.program_id(1)

    @pl.when(c == 0)
    def _():
        state_ref[...] = jnp.zeros_like(state_ref)

    a_row = -jnp.exp(alog_ref[...])
    a_col = -jnp.exp(alogT_ref[...])
    dt = dt_ref[0]
    dtT = dtT_ref[0]
    row = lax.broadcasted_iota(jnp.int32, (BLK, BLK), 0)
    col = lax.broadcasted_iota(jnp.int32, (BLK, BLK), 1)
    tril = row >= col
    tri_l = tril.astype(BF16)
    tri_u = (row <= col).astype(BF16)
    lo = col < HEAD_DIM

    cum = sum(_dot(tri_l, p) for p in _split3(dt * a_row))
    cumT = sum(_dot(p, tri_u) for p in _split3(dtT * a_col))
    ecum = jnp.exp(cum)
    dte = jnp.exp(cum[BLK - 1:BLK, :] - cum) * dt
    ecl = jnp.exp(cumT[:, BLK - 1:BLK])

    for g in range(SSM_GROUPS):
        bg = xbc_ref[0, :, D_SSM + g * D_STATE:D_SSM + (g + 1) * D_STATE]
        cg = xbc_ref[0, :, D_SSM + (SSM_GROUPS + g) * D_STATE:D_SSM + (SSM_GROUPS + g + 1) * D_STATE]
        cb = _dot_nt(cg, bg)
        for jj in range(2):
            j = 2 * g + jj
            h0, h1 = 2 * j, 2 * j + 1
            xp = xbc_ref[0, :, j * LANES:(j + 1) * LANES]
            y_pair = jnp.zeros((BLK, LANES), F32)
            for hm, h in ((lo, h0), (~lo, h1)):
                seg = cum[:, h:h + 1] - cumT[h:h + 1, :]
                dec = jnp.where(tril, jnp.exp(seg), 0.0)
                m = (cb * dec * dtT[h:h + 1, :]).astype(BF16)
                y_pair = y_pair + _dot(m, jnp.where(hm, xp, jnp.zeros_like(xp)))
            s_pair = state_ref[j]
            e_pair = jnp.where(lo, ecum[:, h0:h0 + 1], ecum[:, h1:h1 + 1])
            y_pair = y_pair + _dot_nt(cg, s_pair.astype(BF16)) * e_pair
            w_pair = jnp.where(lo, dte[:, h0:h0 + 1], dte[:, h1:h1 + 1])
            xf = xp.astype(F32)
            upd = _dot_tn((xf * w_pair).astype(BF16), bg)
            scale = jnp.where(row < HEAD_DIM, ecl[h0:h0 + 1, :], ecl[h1:h1 + 1, :])
            state_ref[j] = s_pair * scale + upd
            sl = slice(j * LANES, (j + 1) * LANES)
            ybuf_ref[:, sl] = y_pair + dskip_ref[:, sl] * xf

    gw = D_SSM // SSM_GROUPS
    for g in range(SSM_GROUPS):
        sl = slice(g * gw, (g + 1) * gw)
        y = ybuf_ref[:, sl] * _silu(z_ref[0, :, sl].astype(F32))
        y_ref[0, :, sl] = (_rms(y) * gssm_ref[:, sl]).astype(BF16)

    @pl.when(c == n_chunks - 1)
    def _():
        hlast_ref[0] = state_ref[...]


def _ssd_prompt(xbc, z, dt, dtT, a_log, a_logT, dskip_lane, g_ssm):
    b, s, _ = xbc.shape
    n_chunks = s // BLK
    row = lambda bi, c: (bi, c, 0)
    return pl.pallas_call(
        functools.partial(_ssd_kernel, n_chunks=n_chunks),
        grid=(b, n_chunks),
        in_specs=[
            pl.BlockSpec((1, BLK, D_CONV), row),
            pl.BlockSpec((1, BLK, D_SSM), row),
            pl.BlockSpec((1, BLK, N_HEADS), row),
            pl.BlockSpec((1, N_HEADS, BLK), lambda bi, c: (bi, 0, c)),
            pl.BlockSpec((1, N_HEADS), lambda bi, c: (0, 0)),
            pl.BlockSpec((N_HEADS, 1), lambda bi, c: (0, 0)),
            pl.BlockSpec((1, D_SSM), lambda bi, c: (0, 0)),
            pl.BlockSpec((1, D_SSM), lambda bi, c: (0, 0)),
        ],
        out_specs=[
            pl.BlockSpec((1, BLK, D_SSM), row),
            pl.BlockSpec((1, N_HEADS // 2, LANES, D_STATE), lambda bi, c: (bi, 0, 0, 0)),
        ],
        out_shape=[
            jax.ShapeDtypeStruct((b, s, D_SSM), BF16),
            jax.ShapeDtypeStruct((b, N_HEADS // 2, LANES, D_STATE), F32),
        ],
        scratch_shapes=[pltpu.VMEM((N_HEADS // 2, LANES, D_STATE), F32), pltpu.VMEM((BLK, D_SSM), F32)],
        compiler_params=pltpu.CompilerParams(dimension_semantics=("arbitrary", "arbitrary"),
                                             vmem_limit_bytes=VMEM_LIMIT),
        name="ssd_prompt",
    )(xbc, z, dt, dtT, a_log, a_logT, dskip_lane, g_ssm)


SUPER = MAX_WINDOW
GROUP = 4
TILE_BLOCKS = SUPER // BLK
N_PHASE = TILE_BLOCKS // GROUP


def _attn_stage(t, q_ref, kp_ref, kc_ref, vp_ref, vc_ref, qf, kf, vf):
    qf[...] = q_ref[0].astype(F32)
    kf[SUPER:, :] = kc_ref[0].astype(F32)
    vf[SUPER:, :] = vc_ref[0].astype(F32)

    @pl.when(t == 0)
    def _():
        kf[0:SUPER, :] = jnp.zeros((SUPER, LANES), F32)
        vf[0:SUPER, :] = jnp.zeros((SUPER, LANES), F32)

    @pl.when(t > 0)
    def _():
        kf[0:SUPER, :] = kp_ref[0].astype(F32)
        vf[0:SUPER, :] = vp_ref[0].astype(F32)


def _attn_group(ci, dil, g, t, qf, kf, vf, acc_s, lse_s):
    nblk = SUPER // (BLK * dil)
    qi = lax.broadcasted_iota(jnp.int32, (BLK, 2 * BLK), 0)
    kj = lax.broadcasted_iota(jnp.int32, (BLK, 2 * BLK), 1)
    band = (kj >= qi) & (kj <= qi + BLK)
    in_cur = kj >= BLK
    lo = lax.broadcasted_iota(jnp.int32, (BLK, LANES), 1) < HEAD_DIM
    lo_k = lax.broadcasted_iota(jnp.int32, (2 * BLK, LANES), 1) < HEAD_DIM
    stride = None if dil == 1 else dil
    blocks = []
    for u in range(GROUP):
        if nblk == GROUP:
            r, n = g, u
        elif nblk > GROUP:
            r, n = g // (nblk // GROUP), (g % (nblk // GROUP)) * GROUP + u
        else:
            r, n = g * (GROUP // nblk) + u // nblk, u % nblk
        qs = r + dil * BLK * n
        ks = SUPER + r + dil * BLK * (n - 1)
        qb = qf[pl.ds(qs, BLK, stride=stride), :].astype(BF16)
        kb = kf[pl.ds(ks, 2 * BLK, stride=stride), :].astype(BF16)
        vb = vf[pl.ds(ks, 2 * BLK, stride=stride), :].astype(BF16)
        mask = band & (in_cur | (t > 0) | (n > 0))
        blocks.append((qs, qb, kb, vb, mask))
    stats = []
    for qs, qb, kb, vb, mask in blocks:
        for hm in (lo, ~lo):
            s = jnp.where(mask, _dot_nt(jnp.where(hm, qb, jnp.zeros_like(qb)), kb), NEG_INF)
            stats.append((s, jnp.max(s, axis=1, keepdims=True)))
    probs = []
    for s, m in stats:
        p = jnp.exp(s - m)
        probs.append((p.astype(BF16), jnp.sum(p, axis=1, keepdims=True)))
    for bi, (qs, qb, kb, vb, mask) in enumerate(blocks):
        (p0, l0), (p1, l1) = probs[2 * bi], probs[2 * bi + 1]
        m0, m1 = stats[2 * bi][1], stats[2 * bi + 1][1]
        zero = jnp.zeros_like(vb)
        acc = _dot(p0, jnp.where(lo_k, vb, zero)) + _dot(p1, jnp.where(lo_k, zero, vb))
        rows = pl.ds(qs, BLK, stride=stride)
        acc_s[ci, rows, :] = acc * jnp.where(lo, 1.0 / l0, 1.0 / l1)
        lse_s[ci, rows, :] = jnp.where(lo, m0 + jnp.log(l0), m1 + jnp.log(l1))


def _attn_merge(o_ref, acc_s, lse_s):
    n_cfg = len(DILATED_CONFIGS)
    rc = 256
    for c0 in range(0, SUPER, rc):
        rows = slice(c0, c0 + rc)
        ls = [lse_s[ci, rows, :] for ci in range(n_cfg)]
        mx = functools.reduce(jnp.maximum, ls)
        num = jnp.zeros((rc, LANES), F32)
        den = jnp.zeros((rc, LANES), F32)
        for ci in range(n_cfg):
            a = jnp.exp(ls[ci] - mx)
            num = num + a * acc_s[ci, rows, :]
            den = den + a
        o_ref[0, rows, :] = (num / den).astype(o_ref.dtype)


def _tail_kernel(x_ref, mod_ref, attn_ref, yssm_ref, gattn_ref, gfin_ref, wout_ref, wup_ref, wdown_ref, y_ref,
                 *, sample, tm):
    if sample:
        mod = lambda k: mod_ref[k]
        attn = attn_ref[...].T
    else:
        mod = lambda k: mod_ref[0, k:k + 1, :]
        attn = attn_ref[0].astype(F32)
    x = x_ref[...].reshape(tm, D_MODEL)
    attn_n = (_rms(attn) * gattn_ref[...]).astype(BF16)
    mix = _dot(attn_n, wout_ref[0:D_ATTN, :]) + _dot(yssm_ref[...].reshape(tm, D_SSM), wout_ref[D_ATTN:, :])
    x1 = x + mod(2) * mix
    h2 = (_rms(x1) * (1.0 + mod(4)) + mod(3)).astype(BF16)
    d_ff = wup_ref.shape[1]
    cw = 1024
    acc = jnp.zeros((tm, D_MODEL), F32)
    for cc in range(d_ff // cw):
        u = jnp.maximum(_dot(h2, wup_ref[:, cc * cw:(cc + 1) * cw]), 0.0)
        acc = acc + _dot((u * u).astype(BF16), wdown_ref[cc * cw:(cc + 1) * cw, :])
    x2 = x1 + mod(5) * acc
    y_ref[...] = (_rms(x2) * gfin_ref[...]).reshape(y_ref.shape)


def _tail_prompt(x, mod, attn, y_ssm, g_attn, g_final, w_out, w_up, w_down, tm=512):
    b, s, d = x.shape
    row = lambda bi, i: (bi, i, 0)
    in_specs = [pl.BlockSpec((1, tm, d), row), pl.BlockSpec((1, 6, d), lambda bi, i: (bi, 0, 0)),
                pl.BlockSpec((1, tm, D_ATTN), row), pl.BlockSpec((1, tm, D_SSM), row),
                _const_spec(g_attn.shape), _const_spec(g_final.shape),
                _const_spec(w_out.shape), _const_spec(w_up.shape), _const_spec(w_down.shape)]
    return pl.pallas_call(
        functools.partial(_tail_kernel, sample=False, tm=tm),
        grid=(b, s // tm), in_specs=in_specs,
        out_specs=pl.BlockSpec((1, tm, d), row),
        out_shape=jax.ShapeDtypeStruct((b, s, d), F32),
        compiler_params=pltpu.CompilerParams(dimension_semantics=("arbitrary", "arbitrary"),
                                             vmem_limit_bytes=VMEM_LIMIT),
        name="tail_prompt",
    )(x, mod, attn, y_ssm, g_attn, g_final, w_out, w_up, w_down)


def _tail_sample(x, mod_t, attn, y_ssm, g_attn, g_final, w_out, w_up, w_down):
    m, d = x.shape
    full = lambda shape: pl.BlockSpec(shape, lambda i: (0,) * len(shape))
    in_specs = [full((m, d)), full(mod_t.shape), full((D_ATTN, m)), full((m, D_SSM)),
                full(g_attn.shape), full(g_final.shape), full(w_out.shape), full(w_up.shape), full(w_down.shape)]
    return pl.pallas_call(
        functools.partial(_tail_kernel, sample=True, tm=m),
        grid=(1,), in_specs=in_specs,
        out_specs=full((m, d)),
        out_shape=jax.ShapeDtypeStruct((m, d), F32),
        compiler_params=pltpu.CompilerParams(dimension_semantics=("arbitrary",), vmem_limit_bytes=VMEM_LIMIT),
        name="tail_sample",
    )(x, mod_t, attn, y_ssm, g_attn, g_final, w_out, w_up, w_down)


def _inproj_sample_kernel(x_ref, mod_ref, w_ref, wdt_ref, cos_ref, sin_ref, convst_ref, convw_ref, convb_ref,
                          dtb_ref, qT_ref, kT_ref, vT_ref, z_ref, xbc_ref, xT_ref, dtT_ref, convout_ref):
    m = x_ref.shape[0]
    h = _rms(x_ref[...]) * (1.0 + mod_ref[1]) + mod_ref[0]
    hb = h.astype(BF16)
    cos = cos_ref[...]
    sin = sin_ref[...]
    lane = lax.broadcasted_iota(jnp.int32, (m, LANES), 1)
    lower = (lane % HEAD_DIM) < (HEAD_DIM // 2)
    for c in range(D_ATTN // LANES):
        sl = slice(c * LANES, (c + 1) * LANES)
        qc = _rope_chunk(_dot(hb, w_ref[:, c * LANES:(c + 1) * LANES]), cos, sin, lower)
        qT_ref[sl, :] = (qc * (HEAD_DIM ** -0.5)).T
        kc = _rope_chunk(_dot(hb, w_ref[:, D_ATTN + c * LANES:D_ATTN + (c + 1) * LANES]), cos, sin, lower)
        kT_ref[sl, :] = kc.T
        vT_ref[sl, :] = _dot(hb, w_ref[:, 2 * D_ATTN + c * LANES:2 * D_ATTN + (c + 1) * LANES]).T
    z_ref[...] = _dot(hb, w_ref[:, 3 * D_ATTN:3 * D_ATTN + D_SSM])
    base = 3 * D_ATTN + D_SSM
    xbc = _dot(hb, w_ref[:, base:base + D_CONV])
    acc = convb_ref[...] + convw_ref[CONV_WIDTH - 1:CONV_WIDTH, :] * xbc
    for w in range(CONV_WIDTH - 1):
        acc = acc + convw_ref[w:w + 1, :] * convst_ref[w]
    xc = _silu(acc)
    xbc_ref[...] = xc
    xT_ref[...] = xc[:, :D_SSM].T
    for w in range(CONV_WIDTH - 2):
        convout_ref[w] = convst_ref[w + 1]
    convout_ref[CONV_WIDTH - 2] = xbc
    dt = _softplus(_dot(hb, wdt_ref[...]) + dtb_ref[...])
    dtT_ref[...] = dt.T[:N_HEADS, :]


def _inproj_sample(x, mod_t, w_main, w_dt, cos, sin, conv_state, conv_w, conv_b, dtb_pad):
    m, d = x.shape
    full = lambda shape: pl.BlockSpec(shape, lambda i: (0,) * len(shape))
    ins = (x, mod_t, w_main, w_dt, cos, sin, conv_state, conv_w, conv_b, dtb_pad)
    out_shape = [
        jax.ShapeDtypeStruct((D_ATTN, m), F32), jax.ShapeDtypeStruct((D_ATTN, m), F32),
        jax.ShapeDtypeStruct((D_ATTN, m), F32), jax.ShapeDtypeStruct((m, D_SSM), F32),
        jax.ShapeDtypeStruct((m, D_CONV), F32), jax.ShapeDtypeStruct((D_SSM, m), F32),
        jax.ShapeDtypeStruct((N_HEADS, m), F32), jax.ShapeDtypeStruct((CONV_WIDTH - 1, m, D_CONV), F32),
    ]
    return pl.pallas_call(
        _inproj_sample_kernel, grid=(1,),
        in_specs=[full(a.shape) for a in ins],
        out_specs=[full(o.shape) for o in out_shape],
        out_shape=out_shape,
        compiler_params=pltpu.CompilerParams(dimension_semantics=("arbitrary",), vmem_limit_bytes=VMEM_LIMIT),
        name="inproj_sample",
    )(*ins)


def _shift_heads(b, qT_ref, knT_ref, vnT_ref, kc_ref, vc_ref, oT_ref, ko_ref, vo_ref, *, hpb, win, nb):
    sel = lax.broadcasted_iota(jnp.int32, (HEAD_DIM, nb), 1) == b
    pick = lambda ref, rs: jnp.sum(jnp.where(sel, ref[rs, :], 0.0), axis=1, keepdims=True)
    nt = win // LANES
    lane = lax.broadcasted_iota(jnp.int32, (HEAD_DIM, LANES), 1)
    keep = lane < LANES - 1
    lane1 = lax.broadcasted_iota(jnp.int32, (1, LANES), 1)
    n_cfg = float(len(DILATED_CONFIGS))

    def mult(c):
        w = lane1 + c * LANES
        tot = jnp.zeros((1, LANES), F32)
        for wd, dil in DILATED_CONFIGS:
            tot = tot + ((w >= win - wd) & ((win - w) % dil == 0)).astype(F32)
        return tot

    mults = [mult(c) for c in range(nt)]

    for h in range(hpb):
        rs = slice(h * HEAD_DIM, (h + 1) * HEAD_DIM)
        qc = pick(qT_ref, rs)
        knc = pick(knT_ref, rs)
        vnc = pick(vnT_ref, rs)
        s_self = jnp.sum(qc * knc, axis=0, keepdims=True)
        s_tiles = [None] * nt
        nxt = jnp.broadcast_to(knc, (HEAD_DIM, LANES))
        for c in reversed(range(nt)):
            tl = slice(c * LANES, (c + 1) * LANES)
            kt = kc_ref[0, h, :, tl]
            s_tiles[c] = jnp.where(mults[c] > 0, jnp.sum(kt * qc, axis=0, keepdims=True), NEG_INF)
            r = pltpu.roll(kt, LANES - 1, 1)
            ko_ref[0, h, :, tl] = jnp.where(keep, r, nxt)
            nxt = r
        m = jnp.maximum(jnp.max(functools.reduce(jnp.maximum, s_tiles), axis=1, keepdims=True), s_self)
        p_self = n_cfg * jnp.exp(s_self - m)
        p_tiles = [mults[c] * jnp.exp(s_tiles[c] - m) for c in range(nt)]
        l = jnp.sum(functools.reduce(lambda a, b_: a + b_, p_tiles), axis=1, keepdims=True) + p_self
        acc = jnp.zeros((HEAD_DIM, LANES), F32)
        nxt = jnp.broadcast_to(vnc, (HEAD_DIM, LANES))
        for c in reversed(range(nt)):
            tl = slice(c * LANES, (c + 1) * LANES)
            vt = vc_ref[0, h, :, tl]
            acc = acc + vt * p_tiles[c]
            r = pltpu.roll(vt, LANES - 1, 1)
            vo_ref[0, h, :, tl] = jnp.where(keep, r, nxt)
            nxt = r
        o = jnp.sum(acc, axis=1, keepdims=True) + p_self * vnc
        oT_ref[rs, :] = jnp.where(sel, o / l, oT_ref[rs, :])


def _attn_shift_kernel(qT_ref, knT_ref, vnT_ref, kc_ref, vc_ref, q_ref, kp_ref, kcur_ref, vp_ref, vcur_ref,
                       oT_ref, ko_ref, vo_ref, o_ref, qf, kf, vf, acc_s, lse_s, *, hpb, win, nb, n_tiles):
    b = pl.program_id(1)
    step = pl.program_id(0) * nb + b
    phase = step % N_PHASE
    t = (step // N_PHASE) % n_tiles

    @pl.when(b == 0)
    def _():
        oT_ref[...] = jnp.zeros_like(oT_ref)

    @pl.when(phase == 0)
    def _():
        _attn_stage(t, q_ref, kp_ref, kcur_ref, vp_ref, vcur_ref, qf, kf, vf)

    for ci, (_, dil) in enumerate(DILATED_CONFIGS):
        _attn_group(ci, dil, phase, t, qf, kf, vf, acc_s, lse_s)
    _shift_heads(b, qT_ref, knT_ref, vnT_ref, kc_ref, vc_ref, oT_ref, ko_ref, vo_ref, hpb=hpb, win=win, nb=nb)

    @pl.when(phase == N_PHASE - 1)
    def _():
        _attn_merge(o_ref, acc_s, lse_s)


def _attn_shift(qT, knT, vnT, cache_kT, cache_vT, q, k, v, hpb=8):
    nb, nh, hd, win = cache_kT.shape
    bp, s, d = q.shape
    n_tiles, n_pairs = s // SUPER, d // LANES
    n_steps = (nh // hpb) * nb
    assert bp * n_pairs * n_tiles * N_PHASE == n_steps, "attention tiles must fill the cache-shift grid exactly"
    n_cfg = len(DILATED_CONFIGS)
    colspec = pl.BlockSpec((hpb * hd, nb), lambda g, b: (g, 0))
    cspec = pl.BlockSpec((1, hpb, hd, win), lambda g, b: (b, g, 0, 0))

    def tile_index(g, b, prev):
        a = (g * nb + b) // N_PHASE
        t = a % n_tiles
        return (a // (n_pairs * n_tiles), jnp.maximum(t - 1, 0) if prev else t, (a // n_tiles) % n_pairs)

    cur = lambda g, b: tile_index(g, b, False)
    prev = lambda g, b: tile_index(g, b, True)
    blk = (1, SUPER, LANES)
    return pl.pallas_call(
        functools.partial(_attn_shift_kernel, hpb=hpb, win=win, nb=nb, n_tiles=n_tiles),
        grid=(nh // hpb, nb),
        in_specs=[colspec, colspec, colspec, cspec, cspec,
                  pl.BlockSpec(blk, cur), pl.BlockSpec(blk, prev), pl.BlockSpec(blk, cur),
                  pl.BlockSpec(blk, prev), pl.BlockSpec(blk, cur)],
        out_specs=[colspec, cspec, cspec, pl.BlockSpec(blk, cur)],
        out_shape=[jax.ShapeDtypeStruct((nh * hd, nb), F32),
                   jax.ShapeDtypeStruct(cache_kT.shape, F32), jax.ShapeDtypeStruct(cache_vT.shape, F32),
                   jax.ShapeDtypeStruct((bp, s, d), BF16)],
        scratch_shapes=[pltpu.VMEM((SUPER, LANES), F32), pltpu.VMEM((2 * SUPER, LANES), F32),
                        pltpu.VMEM((2 * SUPER, LANES), F32), pltpu.VMEM((n_cfg, SUPER, LANES), F32),
                        pltpu.VMEM((n_cfg, SUPER, LANES), F32)],
        compiler_params=pltpu.CompilerParams(dimension_semantics=("arbitrary", "arbitrary"),
                                             vmem_limit_bytes=VMEM_LIMIT),
        name="attn_shift",
    )(qT, knT, vnT, cache_kT, cache_vT, q, k, k, v, v)


def _ssm_sample_kernel(st_ref, xT_ref, b_ref, c_ref, dtT_ref, alogT_ref, new_ref, yT_ref, *, nb):
    h = pl.program_id(0)
    a_col = -jnp.exp(alogT_ref[...])
    dtT = dtT_ref[...]
    decT = jnp.exp(dtT * a_col)
    hsel = lax.broadcasted_iota(jnp.int32, (N_HEADS, nb), 0) == h
    dt_h = jnp.sum(jnp.where(hsel, dtT, 0.0), axis=0, keepdims=True)
    dec_h = jnp.sum(jnp.where(hsel, decT, 0.0), axis=0, keepdims=True)
    xdt = xT_ref[...] * dt_h
    lane = lax.broadcasted_iota(jnp.int32, (HEAD_DIM, nb), 1)
    y_acc = jnp.zeros((HEAD_DIM, nb), F32)
    rows_per_batch = 8
    for b0 in range(0, nb, rows_per_batch):
        prods = []
        for bi in range(b0, b0 + rows_per_batch):
            outer = xdt[:, bi:bi + 1] * b_ref[bi:bi + 1, :]
            new = st_ref[0, bi, 0] * dec_h[:, bi:bi + 1] + outer
            new_ref[0, bi, 0] = new
            prods.append(new * c_ref[bi:bi + 1, :])
        cols = [jnp.sum(p, axis=1, keepdims=True) for p in prods]
        for k, ycol in enumerate(cols):
            y_acc = jnp.where(lane == b0 + k, ycol, y_acc)
    yT_ref[...] = y_acc


def _ssm_sample(state, xT, xbc, dtT, a_logT):
    _, nb, nh, hp, n = state.shape
    hg = nh // SSM_GROUPS
    b_blk0 = D_SSM // D_STATE
    return pl.pallas_call(
        functools.partial(_ssm_sample_kernel, nb=nb),
        grid=(nh,),
        in_specs=[pl.BlockSpec((1, nb, 1, hp, n), lambda h: (0, 0, h, 0, 0)),
                  pl.BlockSpec((hp, nb), lambda h: (h, 0)),
                  pl.BlockSpec((nb, D_STATE), lambda h: (0, b_blk0 + h // hg)),
                  pl.BlockSpec((nb, D_STATE), lambda h: (0, b_blk0 + SSM_GROUPS + h // hg)),
                  pl.BlockSpec((nh, nb), lambda h: (0, 0)),
                  pl.BlockSpec((nh, 1), lambda h: (0, 0))],
        out_specs=[pl.BlockSpec((1, nb, 1, hp, n), lambda h: (0, 0, h, 0, 0)),
                   pl.BlockSpec((hp, nb), lambda h: (h, 0))],
        out_shape=[jax.ShapeDtypeStruct(state.shape, F32), jax.ShapeDtypeStruct((nh * hp, nb), F32)],
        compiler_params=pltpu.CompilerParams(dimension_semantics=("arbitrary",), vmem_limit_bytes=VMEM_LIMIT),
        name="ssm_sample",
    )(state, xT, xbc, xbc, dtT, a_logT)


def _ssm_gate_kernel(yT_ref, xs_ref, z_ref, dskip_ref, gssm_ref, o_ref):
    y = yT_ref[...].T + dskip_ref[...] * xs_ref[...]
    y = y * _silu(z_ref[...])
    gw = D_SSM // SSM_GROUPS
    for g in range(SSM_GROUPS):
        sl = slice(g * gw, (g + 1) * gw)
        o_ref[:, sl] = (_rms(y[:, sl]) * gssm_ref[:, sl]).astype(BF16)


def _ssm_gate(yT, xbc, z, dskip_lane, g_ssm):
    m = z.shape[0]
    full = lambda shape: pl.BlockSpec(shape, lambda i: (0,) * len(shape))
    return pl.pallas_call(
        _ssm_gate_kernel, grid=(1,),
        in_specs=[full(yT.shape), pl.BlockSpec((m, D_SSM), lambda i: (0, 0)), full(z.shape),
                  full(dskip_lane.shape), full(g_ssm.shape)],
        out_specs=full((m, D_SSM)),
        out_shape=jax.ShapeDtypeStruct((m, D_SSM), BF16),
        compiler_params=pltpu.CompilerParams(dimension_semantics=("arbitrary",), vmem_limit_bytes=VMEM_LIMIT),
        name="ssm_gate_sample",
    )(yT, xbc, z, dskip_lane, g_ssm)


def _rope_tables(pos):
    half = HEAD_DIM // 2
    inv_freq = ROPE_THETA ** (-jnp.arange(half, dtype=F32) / half)
    ang = pos.astype(F32)[:, None] * inv_freq[None, :]
    cos = jnp.tile(jnp.cos(ang), (1, LANES // half))
    sin = jnp.tile(jnp.sin(ang), (1, LANES // half))
    sign = jnp.where((jnp.arange(LANES) % HEAD_DIM) < half, -1.0, 1.0).astype(F32)
    return cos, sin * sign[None, :]


def kernel(x_prompt, x_sample, c_prompt, c_sample, cache_k_win, cache_v_win, state_conv, state_ssm, w_ada, b_ada,
           w_in, conv_w, conv_b, dt_bias, a_log, d_skip, g_attn, g_ssm, w_out, w_up, w_down, g_final):
    depth = w_ada.shape[0]
    assert depth == 1, "single-layer trunk"
    b, s, d = x_prompt.shape
    nb, t_new, _ = x_sample.shape
    assert t_new == 1
    win = cache_k_win.shape[2]
    assert win == MAX_WINDOW and s % MAX_WINDOW == 0
    past_len = PAST_LEN
    layer = 0

    w_in_l = w_in[layer]
    w_main = w_in_l[:, :D_MAIN].astype(BF16)
    w_dt = jnp.pad(w_in_l[:, D_MAIN:], ((0, 0), (0, LANES - N_HEADS))).astype(BF16)
    dtb_pad = jnp.pad(dt_bias[layer].astype(F32), (0, LANES - N_HEADS))[None, :]
    w_out_b = w_out[layer].astype(BF16)
    w_up_b = w_up[layer].astype(BF16)
    w_down_b = w_down[layer].astype(BF16)
    conv_w_l = conv_w[layer]
    conv_b_l = conv_b[layer][None, :]
    a_log_row = a_log[layer][None, :]
    a_log_col = a_log[layer][:, None]
    dskip_lane = jnp.repeat(d_skip[layer].astype(F32), HEAD_DIM)[None, :]
    g_attn_l = g_attn[layer][None, :]
    g_ssm_l = g_ssm[layer][None, :]
    g_final_l = g_final[None, :]

    pad_rows = (-(nb + b)) % 8
    c_all = jnp.concatenate([c_sample, c_prompt, jnp.zeros((pad_rows, d), F32)], axis=0)
    mod = _ada(c_all, w_ada[layer], b_ada[layer][None, :])
    mod_s = mod[:nb].reshape(nb, 6, d).transpose(1, 0, 2)
    mod_p = mod[nb:nb + b].reshape(b, 6, d)

    cos_p, sin_p = _rope_tables(jnp.arange(s, dtype=jnp.int32))
    q, k, v, z, xbc, dt, dtT, k_win, v_win, conv_p = _inproj_prompt(
        x_prompt, mod_p, w_main, w_dt, cos_p, sin_p, conv_w_l, conv_b_l, dtb_pad)
    y_ssm, h_last = _ssd_prompt(xbc, z, dt, dtT, a_log_row, a_log_col, dskip_lane, g_ssm_l)

    cos_s, sin_s = _rope_tables(past_len + jnp.arange(1, dtype=jnp.int32))
    qT_s, kT_s, vT_s, z_s, xbc_s, xT_s, dtT_s, conv_s = _inproj_sample(
        x_sample.reshape(nb, d), mod_s, w_main, w_dt, cos_s, sin_s, state_conv[layer].transpose(1, 0, 2),
        conv_w_l, conv_b_l, dtb_pad)

    attnT_s, k_winT_s, v_winT_s, attn = _attn_shift(qT_s, kT_s, vT_s,
                                                    cache_k_win[layer].transpose(0, 2, 3, 1),
                                                    cache_v_win[layer].transpose(0, 2, 3, 1), q, k, v)
    y_prompt = _tail_prompt(x_prompt, mod_p, attn, y_ssm, g_attn_l, g_final_l, w_out_b, w_up_b, w_down_b)
    ssm_s, yT_s = _ssm_sample(state_ssm[layer][None], xT_s, xbc_s, dtT_s, a_log_col)
    y_ssm_s = _ssm_gate(yT_s, xbc_s, z_s, dskip_lane, g_ssm_l)
    y_sample = _tail_sample(x_sample.reshape(nb, d), mod_s, attnT_s, y_ssm_s,
                            g_attn_l, g_final_l, w_out_b, w_up_b, w_down_b)

    hd = (N_HEADS, HEAD_DIM)
    return (y_prompt, y_sample.reshape(nb, 1, d),
            k_win.reshape(1, b, -1, *hd), v_win.reshape(1, b, -1, *hd),
            conv_p[None], h_last.reshape(1, b, N_HEADS, HEAD_DIM, D_STATE),
            k_winT_s.transpose(0, 3, 1, 2)[None], v_winT_s.transpose(0, 3, 1, 2)[None],
            conv_s.transpose(1, 0, 2)[None], ssm_s)
```

```python
import functools
import math

import jax
import jax.numpy as jnp
from jax import lax
from jax.experimental import pallas as pl
from jax.experimental.pallas import tpu as pltpu

F32 = jnp.float32
BF16 = jnp.bfloat16

HEAD_DIM = 64
N_HEADS = 16
D_MODEL = 1024
D_ATTN = 1024
D_SSM = 1024
SSM_GROUPS = 4
D_STATE = 128
CONV_WIDTH = 4
D_CONV = D_SSM + 2 * SSM_GROUPS * D_STATE
D_MAIN = 3 * D_ATTN + D_SSM + D_CONV
DILATED_CONFIGS = ((128, 1), (512, 4), (2048, 16))
MAX_WINDOW = 2048
BLK = 128
ROPE_THETA = 10000.0
NEG_INF = -1e30
EPS = 1e-6
LANES = 128
VMEM_LIMIT = 56 * 1024 * 1024
PAST_LEN = 8192


def _dot(a, b):
    return jnp.dot(a, b, preferred_element_type=F32)


def _dot_nt(a, b):
    return lax.dot_general(a, b, (((1,), (1,)), ((), ())), preferred_element_type=F32)


def _dot_tn(a, b):
    return lax.dot_general(a, b, (((0,), (0,)), ((), ())), preferred_element_type=F32)


def _split3(x):
    hi = x.astype(BF16)
    r = x - hi.astype(F32)
    mid = r.astype(BF16)
    lo = (r - mid.astype(F32)).astype(BF16)
    return hi, mid, lo


def _rms(x):
    return x * lax.rsqrt(jnp.mean(x * x, axis=-1, keepdims=True) + EPS)


def _silu(x):
    return x / (1.0 + jnp.exp(-x))


def _softplus(x):
    return jnp.maximum(x, 0.0) + jnp.log1p(jnp.exp(-jnp.abs(x)))


def _rope_chunk(t, cos, sin_signed, lower):
    rot = jnp.where(lower, pltpu.roll(t, 96, 1), pltpu.roll(t, 32, 1))
    return t * cos + rot * sin_signed


def _const_spec(shape):
    n = len(shape)
    return pl.BlockSpec(shape, lambda *_: (0,) * n, pipeline_mode=pl.Buffered(1))


def _ada_kernel(c_ref, w_ref, b_ref, o_ref):
    c = _silu(c_ref[...]).astype(BF16)
    o_ref[...] = _dot(c, w_ref[...].astype(BF16)) + b_ref[...]


def _ada(c_all, w_ada, b_ada):
    m, d = c_all.shape
    n = w_ada.shape[1]
    tn = 1024
    return pl.pallas_call(
        _ada_kernel,
        grid=(n // tn,),
        in_specs=[pl.BlockSpec((m, d), lambda j: (0, 0)),
                  pl.BlockSpec((d, tn), lambda j: (0, j)),
                  pl.BlockSpec((1, tn), lambda j: (0, j))],
        out_specs=pl.BlockSpec((m, tn), lambda j: (0, j)),
        out_shape=jax.ShapeDtypeStruct((m, n), F32),
        compiler_params=pltpu.CompilerParams(dimension_semantics=("arbitrary",), vmem_limit_bytes=VMEM_LIMIT),
        name="ada_mod",
    )(c_all, w_ada, b_ada)


def _inproj_prompt_kernel(x_ref, mod_ref, w_ref, wdt_ref, cos_ref, sin_ref, convw_ref, convb_ref, dtb_ref,
                          q_ref, k_ref, v_ref, z_ref, xbc_ref, dt_ref, dtT_ref, kwin_ref, vwin_ref, convst_ref,
                          buf_ref, *, tm, n_tiles):
    i = pl.program_id(1)
    x = x_ref[0]
    h = _rms(x) * (1.0 + mod_ref[0, 1:2, :]) + mod_ref[0, 0:1, :]
    hb = h.astype(BF16)
    cos = cos_ref[...]
    sin = sin_ref[...]
    lane = lax.broadcasted_iota(jnp.int32, (tm, LANES), 1)
    lower = (lane % HEAD_DIM) < (HEAD_DIM // 2)

    @pl.when(i == 0)
    def _():
        buf_ref[0:8, :] = jnp.zeros((8, D_CONV), F32)

    cw = 256

    def rope_to(refs, scale, c0, r):
        for hc in range(cw // LANES):
            sl = slice(c0 + hc * LANES, c0 + (hc + 1) * LANES)
            t = _rope_chunk(r[:, hc * LANES:(hc + 1) * LANES], cos, sin, lower)
            if scale != 1.0:
                t = t * scale
            refs[0][0, :, sl] = t.astype(BF16)
            for extra in refs[1:]:
                extra[0, :, sl] = t

    def ep_q(c0, r):
        rope_to((q_ref,), HEAD_DIM ** -0.5, c0, r)

    def ep_k(c0, r):
        rope_to((k_ref, kwin_ref), 1.0, c0, r)

    def ep_v(c0, r):
        v_ref[0, :, c0:c0 + cw] = r.astype(BF16)
        vwin_ref[0, :, c0:c0 + cw] = r

    def ep_z(c0, r):
        z_ref[0, :, c0:c0 + cw] = r.astype(BF16)

    def ep_xbc(c0, r):
        sl = slice(c0, c0 + cw)
        buf_ref[8:8 + tm, sl] = r
        acc = jnp.broadcast_to(convb_ref[:, sl], (tm, cw))
        for w in range(CONV_WIDTH):
            acc = acc + convw_ref[w:w + 1, sl] * buf_ref[5 + w:5 + w + tm, sl]
        xbc_ref[0, :, sl] = _silu(acc).astype(BF16)

    regions = ((0, D_ATTN, ep_q), (D_ATTN, D_ATTN, ep_k), (2 * D_ATTN, D_ATTN, ep_v),
               (3 * D_ATTN, D_SSM, ep_z), (3 * D_ATTN + D_SSM, D_CONV, ep_xbc))
    pending = None
    for base, width, ep in regions:
        for c0 in range(0, width, cw):
            r = _dot(hb, w_ref[:, base + c0:base + c0 + cw])
            if pending is not None:
                pending[0](pending[1], pending[2])
            pending = (ep, c0, r)
    dt_raw = _dot(hb, wdt_ref[...])
    pending[0](pending[1], pending[2])

    dt = _softplus(dt_raw + dtb_ref[...])
    dt_ref[0] = dt[:, :N_HEADS]
    dtT_ref[0] = dt.T[:N_HEADS, :]
    buf_ref[0:8, :] = buf_ref[tm:tm + 8, :]

    @pl.when(i == n_tiles - 1)
    def _():
        convst_ref[0] = buf_ref[tm + 5:tm + 8, :]


def _inproj_prompt(x, mod, w_main, w_dt, cos, sin, conv_w, conv_b, dtb_pad, tm=512):
    b, s, d = x.shape
    n_tiles = s // tm
    win = min(MAX_WINDOW, s)
    first_win_tile = (s - win) // tm
    row = lambda bi, i: (bi, i, 0)
    winrow = lambda bi, i: (bi, jnp.maximum(i - first_win_tile, 0), 0)
    kern = functools.partial(_inproj_prompt_kernel, tm=tm, n_tiles=n_tiles)
    out_shape = [
        jax.ShapeDtypeStruct((b, s, D_ATTN), BF16),
        jax.ShapeDtypeStruct((b, s, D_ATTN), BF16),
        jax.ShapeDtypeStruct((b, s, D_ATTN), BF16),
        jax.ShapeDtypeStruct((b, s, D_SSM), BF16),
        jax.ShapeDtypeStruct((b, s, D_CONV), BF16),
        jax.ShapeDtypeStruct((b, s, N_HEADS), F32),
        jax.ShapeDtypeStruct((b, N_HEADS, s), F32),
        jax.ShapeDtypeStruct((b, win, D_ATTN), F32),
        jax.ShapeDtypeStruct((b, win, D_ATTN), F32),
        jax.ShapeDtypeStruct((b, CONV_WIDTH - 1, D_CONV), F32),
    ]
    out_specs = [
        pl.BlockSpec((1, tm, D_ATTN), row),
        pl.BlockSpec((1, tm, D_ATTN), row),
        pl.BlockSpec((1, tm, D_ATTN), row),
        pl.BlockSpec((1, tm, D_SSM), row),
        pl.BlockSpec((1, tm, D_CONV), row),
        pl.BlockSpec((1, tm, N_HEADS), row),
        pl.BlockSpec((1, N_HEADS, tm), lambda bi, i: (bi, 0, i)),
        pl.BlockSpec((1, tm, D_ATTN), winrow),
        pl.BlockSpec((1, tm, D_ATTN), winrow),
        pl.BlockSpec((1, CONV_WIDTH - 1, D_CONV), lambda bi, i: (bi, 0, 0)),
    ]
    in_specs = [
        pl.BlockSpec((1, tm, d), row),
        pl.BlockSpec((1, 6, d), lambda bi, i: (bi, 0, 0)),
        _const_spec(w_main.shape),
        _const_spec(w_dt.shape),
        pl.BlockSpec((tm, LANES), lambda bi, i: (i, 0)),
        pl.BlockSpec((tm, LANES), lambda bi, i: (i, 0)),
        _const_spec(conv_w.shape),
        _const_spec(conv_b.shape),
        _const_spec(dtb_pad.shape),
    ]
    return pl.pallas_call(
        kern, grid=(b, n_tiles), in_specs=in_specs, out_specs=out_specs, out_shape=out_shape,
        scratch_shapes=[pltpu.VMEM((tm + 8, D_CONV), F32)],
        compiler_params=pltpu.CompilerParams(dimension_semantics=("arbitrary", "arbitrary"),
                                             vmem_limit_bytes=VMEM_LIMIT),
        name="inproj_prompt",
    )(x, mod, w_main, w_dt, cos, sin, conv_w, conv_b, dtb_pad)


def _ssd_kernel(xbc_ref, z_ref, dt_ref, dtT_ref, alog_ref, alogT_ref, dskip_ref, gssm_ref,
                y_ref, hlast_ref, state_ref, ybuf_ref, *, n_chunks):
    c = pl.program_id(1)

    @pl.when(c == 0)
    def _():
        state_ref[...] = jnp.zeros_like(state_ref)

    a_row = -jnp.exp(alog_ref[...])
    a_col = -jnp.exp(alogT_ref[...])
    dt = dt_ref[0]
    dtT = dtT_ref[0]
    row = lax.broadcasted_iota(jnp.int32, (BLK, BLK), 0)
    col = lax.broadcasted_iota(jnp.int32, (BLK, BLK), 1)
    tril = row >= col
    tri_l = tril.astype(BF16)
    tri_u = (row <= col).astype(BF16)
    lo = col < HEAD_DIM

    cum = sum(_dot(tri_l, p) for p in _split3(dt * a_row))
    cumT = sum(_dot(p, tri_u) for p in _split3(dtT * a_col))
    ecum = jnp.exp(cum)
    dte = jnp.exp(cum[BLK - 1:BLK, :] - cum) * dt
    ecl = jnp.exp(cumT[:, BLK - 1:BLK])

    hrow = lax.broadcasted_iota(jnp.int32, (N_HEADS, D_SSM), 0)
    hcol = lax.broadcasted_iota(jnp.int32, (N_HEADS, D_SSM), 1)
    sel_hd = (hcol // HEAD_DIM == hrow).astype(BF16)
    spread = lambda v: sum(_dot(p, sel_hd) for p in _split3(v)[:2])
    ecum_b = spread(ecum)
    dte_b = spread(dte)

    for g in range(SSM_GROUPS):
        bg = xbc_ref[0, :, D_SSM + g * D_STATE:D_SSM + (g + 1) * D_STATE]
        cg = xbc_ref[0, :, D_SSM + (SSM_GROUPS + g) * D_STATE:D_SSM + (SSM_GROUPS + g + 1) * D_STATE]
        cb = _dot_nt(cg, bg)
        for jj in range(2):
            j = 2 * g + jj
            h0, h1 = 2 * j, 2 * j + 1
            xp = xbc_ref[0, :, j * LANES:(j + 1) * LANES]
            y_pair = jnp.zeros((BLK, LANES), F32)
            for hm, h in ((lo, h0), (~lo, h1)):
                seg = cum[:, h:h + 1] - cumT[h:h + 1, :]
                dec = jnp.where(tril, jnp.exp(seg), 0.0)
                m = (cb * dec * dtT[h:h + 1, :]).astype(BF16)
                y_pair = y_pair + _dot(m, jnp.where(hm, xp, jnp.zeros_like(xp)))
            s_pair = state_ref[j]
            e_pair = ecum_b[:, j * LANES:(j + 1) * LANES]
            y_pair = y_pair + _dot_nt(cg, s_pair.astype(BF16)) * e_pair
            w_pair = dte_b[:, j * LANES:(j + 1) * LANES]
            xf = xp.astype(F32)
            upd = _dot_tn((xf * w_pair).astype(BF16), bg)
            scale = jnp.where(row < HEAD_DIM, ecl[h0:h0 + 1, :], ecl[h1:h1 + 1, :])
            state_ref[j] = s_pair * scale + upd
            sl = slice(j * LANES, (j + 1) * LANES)
            ybuf_ref[:, sl] = y_pair + dskip_ref[:, sl] * xf

    gw = D_SSM // SSM_GROUPS
    for g in range(SSM_GROUPS):
        sl = slice(g * gw, (g + 1) * gw)
        y = ybuf_ref[:, sl] * _silu(z_ref[0, :, sl].astype(F32))
        y_ref[0, :, sl] = (_rms(y) * gssm_ref[:, sl]).astype(BF16)

    @pl.when(c == n_chunks - 1)
    def _():
        hlast_ref[0] = state_ref[...]


def _ssd_prompt(xbc, z, dt, dtT, a_log, a_logT, dskip_lane, g_ssm):
    b, s, _ = xbc.shape
    n_chunks = s // BLK
    row = lambda bi, c: (bi, c, 0)
    return pl.pallas_call(
        functools.partial(_ssd_kernel, n_chunks=n_chunks),
        grid=(b, n_chunks),
        in_specs=[
            pl.BlockSpec((1, BLK, D_CONV), row),
            pl.BlockSpec((1, BLK, D_SSM), row),
            pl.BlockSpec((1, BLK, N_HEADS), row),
            pl.BlockSpec((1, N_HEADS, BLK), lambda bi, c: (bi, 0, c)),
            pl.BlockSpec((1, N_HEADS), lambda bi, c: (0, 0)),
            pl.BlockSpec((N_HEADS, 1), lambda bi, c: (0, 0)),
            pl.BlockSpec((1, D_SSM), lambda bi, c: (0, 0)),
            pl.BlockSpec((1, D_SSM), lambda bi, c: (0, 0)),
        ],
        out_specs=[
            pl.BlockSpec((1, BLK, D_SSM), row),
            pl.BlockSpec((1, N_HEADS // 2, LANES, D_STATE), lambda bi, c: (bi, 0, 0, 0)),
        ],
        out_shape=[
            jax.ShapeDtypeStruct((b, s, D_SSM), BF16),
            jax.ShapeDtypeStruct((b, N_HEADS // 2, LANES, D_STATE), F32),
        ],
        scratch_shapes=[pltpu.VMEM((N_HEADS // 2, LANES, D_STATE), F32), pltpu.VMEM((BLK, D_SSM), F32)],
        compiler_params=pltpu.CompilerParams(dimension_semantics=("arbitrary", "arbitrary"),
                                             vmem_limit_bytes=VMEM_LIMIT),
        name="ssd_prompt",
    )(xbc, z, dt, dtT, a_log, a_logT, dskip_lane, g_ssm)


SUPER = MAX_WINDOW
GROUP = 4
TILE_BLOCKS = SUPER // BLK
N_PHASE = TILE_BLOCKS // GROUP


def _attn_stage(t, q_ref, kp_ref, kc_ref, vp_ref, vc_ref, qf, kf, vf):
    qf[...] = q_ref[0].astype(F32)
    kf[SUPER:, :] = kc_ref[0].astype(F32)
    vf[SUPER:, :] = vc_ref[0].astype(F32)

    @pl.when(t == 0)
    def _():
        kf[0:SUPER, :] = jnp.zeros((SUPER, LANES), F32)
        vf[0:SUPER, :] = jnp.zeros((SUPER, LANES), F32)

    @pl.when(t > 0)
    def _():
        kf[0:SUPER, :] = kp_ref[0].astype(F32)
        vf[0:SUPER, :] = vp_ref[0].astype(F32)


def _attn_group(ci, dil, g, t, qf, kf, vf, acc_s, lse_s):
    nblk = SUPER // (BLK * dil)
    qi = lax.broadcasted_iota(jnp.int32, (BLK, 2 * BLK), 0)
    kj = lax.broadcasted_iota(jnp.int32, (BLK, 2 * BLK), 1)
    band = (kj >= qi) & (kj <= qi + BLK)
    in_cur = kj >= BLK
    lo = lax.broadcasted_iota(jnp.int32, (BLK, LANES), 1) < HEAD_DIM
    lo_k = lax.broadcasted_iota(jnp.int32, (2 * BLK, LANES), 1) < HEAD_DIM
    stride = None if dil == 1 else dil
    blocks = []
    for u in range(GROUP):
        if nblk == GROUP:
            r, n = g, u
        elif nblk > GROUP:
            r, n = g // (nblk // GROUP), (g % (nblk // GROUP)) * GROUP + u
        else:
            r, n = g * (GROUP // nblk) + u // nblk, u % nblk
        qs = r + dil * BLK * n
        ks = SUPER + r + dil * BLK * (n - 1)
        qb = qf[pl.ds(qs, BLK, stride=stride), :].astype(BF16)
        kb = kf[pl.ds(ks, 2 * BLK, stride=stride), :].astype(BF16)
        vb = vf[pl.ds(ks, 2 * BLK, stride=stride), :].astype(BF16)
        mask = band & (in_cur | (t > 0) | (n > 0))
        blocks.append((qs, qb, kb, vb, mask))
    stats = []
    for qs, qb, kb, vb, mask in blocks:
        for hm in (lo, ~lo):
            s = jnp.where(mask, _dot_nt(jnp.where(hm, qb, jnp.zeros_like(qb)), kb), NEG_INF)
            stats.append((s, jnp.max(s, axis=1, keepdims=True)))
    probs = []
    for s, m in stats:
        p = jnp.exp(s - m)
        probs.append((p.astype(BF16), jnp.sum(p, axis=1, keepdims=True)))
    for bi, (qs, qb, kb, vb, mask) in enumerate(blocks):
        (p0, l0), (p1, l1) = probs[2 * bi], probs[2 * bi + 1]
        m0, m1 = stats[2 * bi][1], stats[2 * bi + 1][1]
        zero = jnp.zeros_like(vb)
        acc = _dot(p0, jnp.where(lo_k, vb, zero)) + _dot(p1, jnp.where(lo_k, zero, vb))
        rows = pl.ds(qs, BLK, stride=stride)
        acc_s[ci, rows, :] = acc * jnp.where(lo, 1.0 / l0, 1.0 / l1)
        lse_s[ci, rows, :] = jnp.where(lo, m0 + jnp.log(l0), m1 + jnp.log(l1))


def _attn_merge(o_ref, acc_s, lse_s):
    n_cfg = len(DILATED_CONFIGS)
    rc = 256
    for c0 in range(0, SUPER, rc):
        rows = slice(c0, c0 + rc)
        ls = [lse_s[ci, rows, :] for ci in range(n_cfg)]
        mx = functools.reduce(jnp.maximum, ls)
        num = jnp.zeros((rc, LANES), F32)
        den = jnp.zeros((rc, LANES), F32)
        for ci in range(n_cfg):
            a = jnp.exp(ls[ci] - mx)
            num = num + a * acc_s[ci, rows, :]
            den = den + a
        o_ref[0, rows, :] = (num / den).astype(o_ref.dtype)


def _tail_kernel(x_ref, mod_ref, attn_ref, yssm_ref, gattn_ref, gfin_ref, wout_ref, wup_ref, wdown_ref, y_ref,
                 *, sample, tm):
    if sample:
        mod = lambda k: mod_ref[k]
        attn = attn_ref[...].T
    else:
        mod = lambda k: mod_ref[0, k:k + 1, :]
        attn = attn_ref[0].astype(F32)
    x = x_ref[...].reshape(tm, D_MODEL)
    attn_n = (_rms(attn) * gattn_ref[...]).astype(BF16)
    mix = _dot(attn_n, wout_ref[0:D_ATTN, :]) + _dot(yssm_ref[...].reshape(tm, D_SSM), wout_ref[D_ATTN:, :])
    x1 = x + mod(2) * mix
    h2 = (_rms(x1) * (1.0 + mod(4)) + mod(3)).astype(BF16)
    d_ff = wup_ref.shape[1]
    cw = 1024
    acc = jnp.zeros((tm, D_MODEL), F32)
    for cc in range(d_ff // cw):
        u = jnp.maximum(_dot(h2, wup_ref[:, cc * cw:(cc + 1) * cw]), 0.0)
        acc = acc + _dot((u * u).astype(BF16), wdown_ref[cc * cw:(cc + 1) * cw, :])
    x2 = x1 + mod(5) * acc
    y_ref[...] = (_rms(x2) * gfin_ref[...]).reshape(y_ref.shape)


def _tail_prompt(x, mod, attn, y_ssm, g_attn, g_final, w_out, w_up, w_down, tm=512):
    b, s, d = x.shape
    row = lambda bi, i: (bi, i, 0)
    in_specs = [pl.BlockSpec((1, tm, d), row), pl.BlockSpec((1, 6, d), lambda bi, i: (bi, 0, 0)),
                pl.BlockSpec((1, tm, D_ATTN), row), pl.BlockSpec((1, tm, D_SSM), row),
                _const_spec(g_attn.shape), _const_spec(g_final.shape),
                _const_spec(w_out.shape), _const_spec(w_up.shape), _const_spec(w_down.shape)]
    return pl.pallas_call(
        functools.partial(_tail_kernel, sample=False, tm=tm),
        grid=(b, s // tm), in_specs=in_specs,
        out_specs=pl.BlockSpec((1, tm, d), row),
        out_shape=jax.ShapeDtypeStruct((b, s, d), F32),
        compiler_params=pltpu.CompilerParams(dimension_semantics=("arbitrary", "arbitrary"),
                                             vmem_limit_bytes=VMEM_LIMIT),
        name="tail_prompt",
    )(x, mod, attn, y_ssm, g_attn, g_final, w_out, w_up, w_down)


def _tail_sample(x, mod_t, attn, y_ssm, g_attn, g_final, w_out, w_up, w_down):
    m, d = x.shape
    full = lambda shape: pl.BlockSpec(shape, lambda i: (0,) * len(shape))
    in_specs = [full((m, d)), full(mod_t.shape), full((D_ATTN, m)), full((m, D_SSM)),
                full(g_attn.shape), full(g_final.shape), full(w_out.shape), full(w_up.shape), full(w_down.shape)]
    return pl.pallas_call(
        functools.partial(_tail_kernel, sample=True, tm=m),
        grid=(1,), in_specs=in_specs,
        out_specs=full((m, d)),
        out_shape=jax.ShapeDtypeStruct((m, d), F32),
        compiler_params=pltpu.CompilerParams(dimension_semantics=("arbitrary",), vmem_limit_bytes=VMEM_LIMIT),
        name="tail_sample",
    )(x, mod_t, attn, y_ssm, g_attn, g_final, w_out, w_up, w_down)


def _inproj_sample_kernel(x_ref, mod_ref, w_ref, wdt_ref, cos_ref, sin_ref, convst_ref, convw_ref, convb_ref,
                          dtb_ref, qT_ref, kT_ref, vT_ref, z_ref, xbc_ref, xT_ref, dtT_ref, convout_ref):
    m = x_ref.shape[0]
    h = _rms(x_ref[...]) * (1.0 + mod_ref[1]) + mod_ref[0]
    hb = h.astype(BF16)
    cos = cos_ref[...]
    sin = sin_ref[...]
    lane = lax.broadcasted_iota(jnp.int32, (m, LANES), 1)
    lower = (lane % HEAD_DIM) < (HEAD_DIM // 2)
    for c in range(D_ATTN // LANES):
        sl = slice(c * LANES, (c + 1) * LANES)
        qc = _rope_chunk(_dot(hb, w_ref[:, c * LANES:(c + 1) * LANES]), cos, sin, lower)
        qT_ref[sl, :] = (qc * (HEAD_DIM ** -0.5)).T
        kc = _rope_chunk(_dot(hb, w_ref[:, D_ATTN + c * LANES:D_ATTN + (c + 1) * LANES]), cos, sin, lower)
        kT_ref[sl, :] = kc.T
        vT_ref[sl, :] = _dot(hb, w_ref[:, 2 * D_ATTN + c * LANES:2 * D_ATTN + (c + 1) * LANES]).T
    z_ref[...] = _dot(hb, w_ref[:, 3 * D_ATTN:3 * D_ATTN + D_SSM])
    base = 3 * D_ATTN + D_SSM
    xbc = _dot(hb, w_ref[:, base:base + D_CONV])
    acc = convb_ref[...] + convw_ref[CONV_WIDTH - 1:CONV_WIDTH, :] * xbc
    for w in range(CONV_WIDTH - 1):
        acc = acc + convw_ref[w:w + 1, :] * convst_ref[w]
    xc = _silu(acc)
    xbc_ref[...] = xc
    xT_ref[...] = xc[:, :D_SSM].T
    for w in range(CONV_WIDTH - 2):
        convout_ref[w] = convst_ref[w + 1]
    convout_ref[CONV_WIDTH - 2] = xbc
    dt = _softplus(_dot(hb, wdt_ref[...]) + dtb_ref[...])
    dtT_ref[...] = dt.T[:N_HEADS, :]


def _inproj_sample(x, mod_t, w_main, w_dt, cos, sin, conv_state, conv_w, conv_b, dtb_pad):
    m, d = x.shape
    full = lambda shape: pl.BlockSpec(shape, lambda i: (0,) * len(shape))
    ins = (x, mod_t, w_main, w_dt, cos, sin, conv_state, conv_w, conv_b, dtb_pad)
    out_shape = [
        jax.ShapeDtypeStruct((D_ATTN, m), F32), jax.ShapeDtypeStruct((D_ATTN, m), F32),
        jax.ShapeDtypeStruct((D_ATTN, m), F32), jax.ShapeDtypeStruct((m, D_SSM), F32),
        jax.ShapeDtypeStruct((m, D_CONV), F32), jax.ShapeDtypeStruct((D_SSM, m), F32),
        jax.ShapeDtypeStruct((N_HEADS, m), F32), jax.ShapeDtypeStruct((CONV_WIDTH - 1, m, D_CONV), F32),
    ]
    return pl.pallas_call(
        _inproj_sample_kernel, grid=(1,),
        in_specs=[full(a.shape) for a in ins],
        out_specs=[full(o.shape) for o in out_shape],
        out_shape=out_shape,
        compiler_params=pltpu.CompilerParams(dimension_semantics=("arbitrary",), vmem_limit_bytes=VMEM_LIMIT),
        name="inproj_sample",
    )(*ins)


def _shift_heads(b, heads, qT_ref, knT_ref, vnT_ref, kc_ref, vc_ref, oT_ref, ko_ref, vo_ref, *, win, nb):
    sel = lax.broadcasted_iota(jnp.int32, (HEAD_DIM, nb), 1) == b
    pick = lambda ref, rs: jnp.sum(jnp.where(sel, ref[rs, :], 0.0), axis=1, keepdims=True)
    nt = win // LANES
    lane = lax.broadcasted_iota(jnp.int32, (HEAD_DIM, LANES), 1)
    keep = lane < LANES - 1
    lane1 = lax.broadcasted_iota(jnp.int32, (1, LANES), 1)
    n_cfg = float(len(DILATED_CONFIGS))

    def mult(c):
        w = lane1 + c * LANES
        tot = jnp.zeros((1, LANES), F32)
        for wd, dil in DILATED_CONFIGS:
            tot = tot + ((w >= win - wd) & ((win - w) % dil == 0)).astype(F32)
        return tot

    mults = [mult(c) for c in range(nt)]

    for h in heads:
        rs = slice(h * HEAD_DIM, (h + 1) * HEAD_DIM)
        qc = pick(qT_ref, rs)
        knc = pick(knT_ref, rs)
        vnc = pick(vnT_ref, rs)
        s_self = jnp.sum(qc * knc, axis=0, keepdims=True)
        s_tiles = [None] * nt
        nxt = jnp.broadcast_to(knc, (HEAD_DIM, LANES))
        for c in reversed(range(nt)):
            tl = slice(c * LANES, (c + 1) * LANES)
            kt = kc_ref[0, h, :, tl]
            s_tiles[c] = jnp.where(mults[c] > 0, jnp.sum(kt * qc, axis=0, keepdims=True), NEG_INF)
            r = pltpu.roll(kt, LANES - 1, 1)
            ko_ref[0, h, :, tl] = jnp.where(keep, r, nxt)
            nxt = r
        m = jnp.maximum(jnp.max(functools.reduce(jnp.maximum, s_tiles), axis=1, keepdims=True), s_self)
        p_self = n_cfg * jnp.exp(s_self - m)
        p_tiles = [mults[c] * jnp.exp(s_tiles[c] - m) for c in range(nt)]
        l = jnp.sum(functools.reduce(lambda a, b_: a + b_, p_tiles), axis=1, keepdims=True) + p_self
        acc = jnp.zeros((HEAD_DIM, LANES), F32)
        nxt = jnp.broadcast_to(vnc, (HEAD_DIM, LANES))
        for c in reversed(range(nt)):
            tl = slice(c * LANES, (c + 1) * LANES)
            vt = vc_ref[0, h, :, tl]
            acc = acc + vt * p_tiles[c]
            r = pltpu.roll(vt, LANES - 1, 1)
            vo_ref[0, h, :, tl] = jnp.where(keep, r, nxt)
            nxt = r
        o = jnp.sum(acc, axis=1, keepdims=True) + p_self * vnc
        oT_ref[rs, :] = jnp.where(sel, o / l, oT_ref[rs, :])


def _attn_shift_kernel(qT_ref, knT_ref, vnT_ref, kc_ref, vc_ref, q_ref, kp_ref, kcur_ref, vp_ref, vcur_ref,
                       oT_ref, ko_ref, vo_ref, o_ref, qf, kf, vf, acc_s, lse_s, *, hpb, win, nb, n_tiles):
    b = pl.program_id(1)
    step = pl.program_id(0) * nb + b
    phase = step % N_PHASE
    t = (step // N_PHASE) % n_tiles

    @pl.when(b == 0)
    def _():
        oT_ref[...] = jnp.zeros_like(oT_ref)

    @pl.when(phase == 0)
    def _():
        _attn_stage(t, q_ref, kp_ref, kcur_ref, vp_ref, vcur_ref, qf, kf, vf)

    n_cfg = len(DILATED_CONFIGS)
    for ci, (_, dil) in enumerate(DILATED_CONFIGS):
        _attn_group(ci, dil, phase, t, qf, kf, vf, acc_s, lse_s)
        heads = range(ci * hpb // n_cfg, (ci + 1) * hpb // n_cfg)
        _shift_heads(b, heads, qT_ref, knT_ref, vnT_ref, kc_ref, vc_ref, oT_ref, ko_ref, vo_ref, win=win, nb=nb)

    @pl.when(phase == N_PHASE - 1)
    def _():
        _attn_merge(o_ref, acc_s, lse_s)


def _attn_shift(qT, knT, vnT, cache_kT, cache_vT, q, k, v, hpb=8):
    nb, nh, hd, win = cache_kT.shape
    bp, s, d = q.shape
    n_tiles, n_pairs = s // SUPER, d // LANES
    n_steps = (nh // hpb) * nb
    assert bp * n_pairs * n_tiles * N_PHASE == n_steps, "attention tiles must fill the cache-shift grid exactly"
    n_cfg = len(DILATED_CONFIGS)
    colspec = pl.BlockSpec((hpb * hd, nb), lambda g, b: (g, 0))
    cspec = pl.BlockSpec((1, hpb, hd, win), lambda g, b: (b, g, 0, 0))

    def tile_index(g, b, prev):
        a = (g * nb + b) // N_PHASE
        t = a % n_tiles
        return (a // (n_pairs * n_tiles), jnp.maximum(t - 1, 0) if prev else t, (a // n_tiles) % n_pairs)

    cur = lambda g, b: tile_index(g, b, False)
    prev = lambda g, b: tile_index(g, b, True)
    blk = (1, SUPER, LANES)
    return pl.pallas_call(
        functools.partial(_attn_shift_kernel, hpb=hpb, win=win, nb=nb, n_tiles=n_tiles),
        grid=(nh // hpb, nb),
        in_specs=[colspec, colspec, colspec, cspec, cspec,
                  pl.BlockSpec(blk, cur), pl.BlockSpec(blk, prev), pl.BlockSpec(blk, cur),
                  pl.BlockSpec(blk, prev), pl.BlockSpec(blk, cur)],
        out_specs=[colspec, cspec, cspec, pl.BlockSpec(blk, cur)],
        out_shape=[jax.ShapeDtypeStruct((nh * hd, nb), F32),
                   jax.ShapeDtypeStruct(cache_kT.shape, F32), jax.ShapeDtypeStruct(cache_vT.shape, F32),
                   jax.ShapeDtypeStruct((bp, s, d), BF16)],
        scratch_shapes=[pltpu.VMEM((SUPER, LANES), F32), pltpu.VMEM((2 * SUPER, LANES), F32),
                        pltpu.VMEM((2 * SUPER, LANES), F32), pltpu.VMEM((n_cfg, SUPER, LANES), F32),
                        pltpu.VMEM((n_cfg, SUPER, LANES), F32)],
        compiler_params=pltpu.CompilerParams(dimension_semantics=("arbitrary", "arbitrary"),
                                             vmem_limit_bytes=VMEM_LIMIT),
        name="attn_shift",
    )(qT, knT, vnT, cache_kT, cache_vT, q, k, k, v, v)


def _ssm_sample_kernel(st_ref, xT_ref, b_ref, c_ref, dtT_ref, alogT_ref, new_ref, yT_ref, *, nb):
    h = pl.program_id(0)
    a_col = -jnp.exp(alogT_ref[...])
    dtT = dtT_ref[...]
    decT = jnp.exp(dtT * a_col)
    hsel = lax.broadcasted_iota(jnp.int32, (N_HEADS, nb), 0) == h
    dt_h = jnp.sum(jnp.where(hsel, dtT, 0.0), axis=0, keepdims=True)
    dec_h = jnp.sum(jnp.where(hsel, decT, 0.0), axis=0, keepdims=True)
    xdt = xT_ref[...] * dt_h
    lane = lax.broadcasted_iota(jnp.int32, (HEAD_DIM, nb), 1)
    y_acc = jnp.zeros((HEAD_DIM, nb), F32)
    rows_per_batch = 8
    for b0 in range(0, nb, rows_per_batch):
        prods = []
        for bi in range(b0, b0 + rows_per_batch):
            outer = xdt[:, bi:bi + 1] * b_ref[bi:bi + 1, :]
            new = st_ref[0, bi, 0] * dec_h[:, bi:bi + 1] + outer
            new_ref[0, bi, 0] = new
            prods.append(new * c_ref[bi:bi + 1, :])
        cols = [jnp.sum(p, axis=1, keepdims=True) for p in prods]
        for k, ycol in enumerate(cols):
            y_acc = jnp.where(lane == b0 + k, ycol, y_acc)
    yT_ref[...] = y_acc


def _ssm_sample(state, xT, xbc, dtT, a_logT):
    _, nb, nh, hp, n = state.shape
    hg = nh // SSM_GROUPS
    b_blk0 = D_SSM // D_STATE
    return pl.pallas_call(
        functools.partial(_ssm_sample_kernel, nb=nb),
        grid=(nh,),
        in_specs=[pl.BlockSpec((1, nb, 1, hp, n), lambda h: (0, 0, h, 0, 0)),
                  pl.BlockSpec((hp, nb), lambda h: (h, 0)),
                  pl.BlockSpec((nb, D_STATE), lambda h: (0, b_blk0 + h // hg)),
                  pl.BlockSpec((nb, D_STATE), lambda h: (0, b_blk0 + SSM_GROUPS + h // hg)),
                  pl.BlockSpec((nh, nb), lambda h: (0, 0)),
                  pl.BlockSpec((nh, 1), lambda h: (0, 0))],
        out_specs=[pl.BlockSpec((1, nb, 1, hp, n), lambda h: (0, 0, h, 0, 0)),
                   pl.BlockSpec((hp, nb), lambda h: (h, 0))],
        out_shape=[jax.ShapeDtypeStruct(state.shape, F32), jax.ShapeDtypeStruct((nh * hp, nb), F32)],
        compiler_params=pltpu.CompilerParams(dimension_semantics=("arbitrary",), vmem_limit_bytes=VMEM_LIMIT),
        name="ssm_sample",
    )(state, xT, xbc, xbc, dtT, a_logT)


def _ssm_gate_kernel(yT_ref, xs_ref, z_ref, dskip_ref, gssm_ref, o_ref):
    y = yT_ref[...].T + dskip_ref[...] * xs_ref[...]
    y = y * _silu(z_ref[...])
    gw = D_SSM // SSM_GROUPS
    for g in range(SSM_GROUPS):
        sl = slice(g * gw, (g + 1) * gw)
        o_ref[:, sl] = (_rms(y[:, sl]) * gssm_ref[:, sl]).astype(BF16)


def _ssm_gate(yT, xbc, z, dskip_lane, g_ssm):
    m = z.shape[0]
    full = lambda shape: pl.BlockSpec(shape, lambda i: (0,) * len(shape))
    return pl.pallas_call(
        _ssm_gate_kernel, grid=(1,),
        in_specs=[full(yT.shape), pl.BlockSpec((m, D_SSM), lambda i: (0, 0)), full(z.shape),
                  full(dskip_lane.shape), full(g_ssm.shape)],
        out_specs=full((m, D_SSM)),
        out_shape=jax.ShapeDtypeStruct((m, D_SSM), BF16),
        compiler_params=pltpu.CompilerParams(dimension_semantics=("arbitrary",), vmem_limit_bytes=VMEM_LIMIT),
        name="ssm_gate_sample",
    )(yT, xbc, z, dskip_lane, g_ssm)


def _rope_tables(pos):
    half = HEAD_DIM // 2
    inv_freq = ROPE_THETA ** (-jnp.arange(half, dtype=F32) / half)
    ang = pos.astype(F32)[:, None] * inv_freq[None, :]
    cos = jnp.tile(jnp.cos(ang), (1, LANES // half))
    sin = jnp.tile(jnp.sin(ang), (1, LANES // half))
    sign = jnp.where((jnp.arange(LANES) % HEAD_DIM) < half, -1.0, 1.0).astype(F32)
    return cos, sin * sign[None, :]


def kernel(x_prompt, x_sample, c_prompt, c_sample, cache_k_win, cache_v_win, state_conv, state_ssm, w_ada, b_ada,
           w_in, conv_w, conv_b, dt_bias, a_log, d_skip, g_attn, g_ssm, w_out, w_up, w_down, g_final):
    depth = w_ada.shape[0]
    assert depth == 1, "single-layer trunk"
    b, s, d = x_prompt.shape
    nb, t_new, _ = x_sample.shape
    assert t_new == 1
    win = cache_k_win.shape[2]
    assert win == MAX_WINDOW and s % MAX_WINDOW == 0
    past_len = PAST_LEN
    layer = 0

    w_in_l = w_in[layer]
    w_main = w_in_l[:, :D_MAIN].astype(BF16)
    w_dt = jnp.pad(w_in_l[:, D_MAIN:], ((0, 0), (0, LANES - N_HEADS))).astype(BF16)
    dtb_pad = jnp.pad(dt_bias[layer].astype(F32), (0, LANES - N_HEADS))[None, :]
    w_out_b = w_out[layer].astype(BF16)
    w_up_b = w_up[layer].astype(BF16)
    w_down_b = w_down[layer].astype(BF16)
    conv_w_l = conv_w[layer]
    conv_b_l = conv_b[layer][None, :]
    a_log_row = a_log[layer][None, :]
    a_log_col = a_log[layer][:, None]
    dskip_lane = jnp.repeat(d_skip[layer].astype(F32), HEAD_DIM)[None, :]
    g_attn_l = g_attn[layer][None, :]
    g_ssm_l = g_ssm[layer][None, :]
    g_final_l = g_final[None, :]

    pad_rows = (-(nb + b)) % 8
    c_all = jnp.concatenate([c_sample, c_prompt, jnp.zeros((pad_rows, d), F32)], axis=0)
    mod = _ada(c_all, w_ada[layer], b_ada[layer][None, :])
    mod_s = mod[:nb].reshape(nb, 6, d).transpose(1, 0, 2)
    mod_p = mod[nb:nb + b].reshape(b, 6, d)

    cos_p, sin_p = _rope_tables(jnp.arange(s, dtype=jnp.int32))
    q, k, v, z, xbc, dt, dtT, k_win, v_win, conv_p = _inproj_prompt(
        x_prompt, mod_p, w_main, w_dt, cos_p, sin_p, conv_w_l, conv_b_l, dtb_pad)
    y_ssm, h_last = _ssd_prompt(xbc, z, dt, dtT, a_log_row, a_log_col, dskip_lane, g_ssm_l)

    cos_s, sin_s = _rope_tables(past_len + jnp.arange(1, dtype=jnp.int32))
    qT_s, kT_s, vT_s, z_s, xbc_s, xT_s, dtT_s, conv_s = _inproj_sample(
        x_sample.reshape(nb, d), mod_s, w_main, w_dt, cos_s, sin_s, state_conv[layer].transpose(1, 0, 2),
        conv_w_l, conv_b_l, dtb_pad)

    attnT_s, k_winT_s, v_winT_s, attn = _attn_shift(qT_s, kT_s, vT_s,
                                                    cache_k_win[layer].transpose(0, 2, 3, 1),
                                                    cache_v_win[layer].transpose(0, 2, 3, 1), q, k, v)
    y_prompt = _tail_prompt(x_prompt, mod_p, attn, y_ssm, g_attn_l, g_final_l, w_out_b, w_up_b, w_down_b)
    ssm_s, yT_s = _ssm_sample(state_ssm[layer][None], xT_s, xbc_s, dtT_s, a_log_col)
    y_ssm_s = _ssm_gate(yT_s, xbc_s, z_s, dskip_lane, g_ssm_l)
    y_sample = _tail_sample(x_sample.reshape(nb, d), mod_s, attnT_s, y_ssm_s,
                            g_attn_l, g_final_l, w_out_b, w_up_b, w_down_b)

    hd = (N_HEADS, HEAD_DIM)
    return (y_prompt, y_sample.reshape(nb, 1, d),
            k_win.reshape(1, b, -1, *hd), v_win.reshape(1, b, -1, *hd),
            conv_p[None], h_last.reshape(1, b, N_HEADS, HEAD_DIM, D_STATE),
            k_winT_s.transpose(0, 3, 1, 2)[None], v_winT_s.transpose(0, 3, 1, 2)[None],
            conv_s.transpose(1, 0, 2)[None], ssm_s)
```

```python
import functools
import math

import jax
import jax.numpy as jnp
from jax import lax
from jax.experimental import pallas as pl
from jax.experimental.pallas import tpu as pltpu

F32 = jnp.float32
BF16 = jnp.bfloat16

HEAD_DIM = 64
N_HEADS = 16
D_MODEL = 1024
D_ATTN = 1024
D_SSM = 1024
SSM_GROUPS = 4
D_STATE = 128
CONV_WIDTH = 4
D_CONV = D_SSM + 2 * SSM_GROUPS * D_STATE
D_MAIN = 3 * D_ATTN + D_SSM + D_CONV
DILATED_CONFIGS = ((128, 1), (512, 4), (2048, 16))
MAX_WINDOW = 2048
BLK = 128
ROPE_THETA = 10000.0
NEG_INF = -1e30
EPS = 1e-6
LANES = 128
VMEM_LIMIT = 56 * 1024 * 1024
PAST_LEN = 8192


def _dot(a, b):
    return jnp.dot(a, b, preferred_element_type=F32)


def _dot_nt(a, b):
    return lax.dot_general(a, b, (((1,), (1,)), ((), ())), preferred_element_type=F32)


def _dot_tn(a, b):
    return lax.dot_general(a, b, (((0,), (0,)), ((), ())), preferred_element_type=F32)


def _split3(x):
    hi = x.astype(BF16)
    r = x - hi.astype(F32)
    mid = r.astype(BF16)
    lo = (r - mid.astype(F32)).astype(BF16)
    return hi, mid, lo


def _rms(x):
    return x * lax.rsqrt(jnp.mean(x * x, axis=-1, keepdims=True) + EPS)


def _silu(x):
    return x / (1.0 + jnp.exp(-x))


def _softplus(x):
    return jnp.maximum(x, 0.0) + jnp.log1p(jnp.exp(-jnp.abs(x)))


def _rope_chunk(t, cos, sin_signed, lower):
    rot = jnp.where(lower, pltpu.roll(t, 96, 1), pltpu.roll(t, 32, 1))
    return t * cos + rot * sin_signed


def _const_spec(shape):
    n = len(shape)
    return pl.BlockSpec(shape, lambda *_: (0,) * n, pipeline_mode=pl.Buffered(1))


def _ada_kernel(c_ref, w_ref, b_ref, o_ref):
    c = _silu(c_ref[...]).astype(BF16)
    o_ref[...] = _dot(c, w_ref[...].astype(BF16)) + b_ref[...]


def _ada(c_all, w_ada, b_ada):
    m, d = c_all.shape
    n = w_ada.shape[1]
    tn = 1024
    return pl.pallas_call(
        _ada_kernel,
        grid=(n // tn,),
        in_specs=[pl.BlockSpec((m, d), lambda j: (0, 0)),
                  pl.BlockSpec((d, tn), lambda j: (0, j)),
                  pl.BlockSpec((1, tn), lambda j: (0, j))],
        out_specs=pl.BlockSpec((m, tn), lambda j: (0, j)),
        out_shape=jax.ShapeDtypeStruct((m, n), F32),
        compiler_params=pltpu.CompilerParams(dimension_semantics=("arbitrary",), vmem_limit_bytes=VMEM_LIMIT),
        name="ada_mod",
    )(c_all, w_ada, b_ada)


def _inproj_prompt_kernel(x_ref, mod_ref, w_ref, wdt_ref, cos_ref, sin_ref, convw_ref, convb_ref, dtb_ref,
                          q_ref, k_ref, v_ref, z_ref, xbc_ref, dt_ref, dtT_ref, kwin_ref, vwin_ref, convst_ref,
                          buf_ref, *, tm, n_tiles):
    i = pl.program_id(1)
    x = x_ref[0]
    h = _rms(x) * (1.0 + mod_ref[0, 1:2, :]) + mod_ref[0, 0:1, :]
    hb = h.astype(BF16)
    cos = cos_ref[...]
    sin = sin_ref[...]
    lane = lax.broadcasted_iota(jnp.int32, (tm, LANES), 1)
    lower = (lane % HEAD_DIM) < (HEAD_DIM // 2)

    @pl.when(i == 0)
    def _():
        buf_ref[0:8, :] = jnp.zeros((8, D_CONV), F32)

    cw = 256

    def rope_to(refs, scale, c0, r):
        for hc in range(cw // LANES):
            sl = slice(c0 + hc * LANES, c0 + (hc + 1) * LANES)
            t = _rope_chunk(r[:, hc * LANES:(hc + 1) * LANES], cos, sin, lower)
            if scale != 1.0:
                t = t * scale
            refs[0][0, :, sl] = t.astype(BF16)
            for extra in refs[1:]:
                extra[0, :, sl] = t

    def ep_q(c0, r):
        rope_to((q_ref,), HEAD_DIM ** -0.5, c0, r)

    def ep_k(c0, r):
        rope_to((k_ref, kwin_ref), 1.0, c0, r)

    def ep_v(c0, r):
        v_ref[0, :, c0:c0 + cw] = r.astype(BF16)
        vwin_ref[0, :, c0:c0 + cw] = r

    def ep_z(c0, r):
        z_ref[0, :, c0:c0 + cw] = r.astype(BF16)

    def ep_xbc(c0, r):
        sl = slice(c0, c0 + cw)
        buf_ref[8:8 + tm, sl] = r
        acc = jnp.broadcast_to(convb_ref[:, sl], (tm, cw))
        for w in range(CONV_WIDTH):
            acc = acc + convw_ref[w:w + 1, sl] * buf_ref[5 + w:5 + w + tm, sl]
        xbc_ref[0, :, sl] = _silu(acc).astype(BF16)

    regions = ((0, D_ATTN, ep_q), (D_ATTN, D_ATTN, ep_k), (2 * D_ATTN, D_ATTN, ep_v),
               (3 * D_ATTN, D_SSM, ep_z), (3 * D_ATTN + D_SSM, D_CONV, ep_xbc))
    pending = None
    for base, width, ep in regions:
        for c0 in range(0, width, cw):
            r = _dot(hb, w_ref[:, base + c0:base + c0 + cw])
            if pending is not None:
                pending[0](pending[1], pending[2])
            pending = (ep, c0, r)
    dt_raw = _dot(hb, wdt_ref[...])
    pending[0](pending[1], pending[2])

    dt = _softplus(dt_raw + dtb_ref[...])
    dt_ref[0] = dt[:, :N_HEADS]
    dtT_ref[0] = dt.T[:N_HEADS, :]
    buf_ref[0:8, :] = buf_ref[tm:tm + 8, :]

    @pl.when(i == n_tiles - 1)
    def _():
        convst_ref[0] = buf_ref[tm + 5:tm + 8, :]


def _inproj_prompt(x, mod, w_main, w_dt, cos, sin, conv_w, conv_b, dtb_pad, tm=512):
    b, s, d = x.shape
    n_tiles = s // tm
    win = min(MAX_WINDOW, s)
    first_win_tile = (s - win) // tm
    row = lambda bi, i: (bi, i, 0)
    winrow = lambda bi, i: (bi, jnp.maximum(i - first_win_tile, 0), 0)
    kern = functools.partial(_inproj_prompt_kernel, tm=tm, n_tiles=n_tiles)
    out_shape = [
        jax.ShapeDtypeStruct((b, s, D_ATTN), BF16),
        jax.ShapeDtypeStruct((b, s, D_ATTN), BF16),
        jax.ShapeDtypeStruct((b, s, D_ATTN), BF16),
        jax.ShapeDtypeStruct((b, s, D_SSM), BF16),
        jax.ShapeDtypeStruct((b, s, D_CONV), BF16),
        jax.ShapeDtypeStruct((b, s, N_HEADS), F32),
        jax.ShapeDtypeStruct((b, N_HEADS, s), F32),
        jax.ShapeDtypeStruct((b, win, D_ATTN), F32),
        jax.ShapeDtypeStruct((b, win, D_ATTN), F32),
        jax.ShapeDtypeStruct((b, CONV_WIDTH - 1, D_CONV), F32),
    ]
    out_specs = [
        pl.BlockSpec((1, tm, D_ATTN), row),
        pl.BlockSpec((1, tm, D_ATTN), row),
        pl.BlockSpec((1, tm, D_ATTN), row),
        pl.BlockSpec((1, tm, D_SSM), row),
        pl.BlockSpec((1, tm, D_CONV), row),
        pl.BlockSpec((1, tm, N_HEADS), row),
        pl.BlockSpec((1, N_HEADS, tm), lambda bi, i: (bi, 0, i)),
        pl.BlockSpec((1, tm, D_ATTN), winrow),
        pl.BlockSpec((1, tm, D_ATTN), winrow),
        pl.BlockSpec((1, CONV_WIDTH - 1, D_CONV), lambda bi, i: (bi, 0, 0)),
    ]
    in_specs = [
        pl.BlockSpec((1, tm, d), row),
        pl.BlockSpec((1, 6, d), lambda bi, i: (bi, 0, 0)),
        _const_spec(w_main.shape),
        _const_spec(w_dt.shape),
        pl.BlockSpec((tm, LANES), lambda bi, i: (i, 0)),
        pl.BlockSpec((tm, LANES), lambda bi, i: (i, 0)),
        _const_spec(conv_w.shape),
        _const_spec(conv_b.shape),
        _const_spec(dtb_pad.shape),
    ]
    return pl.pallas_call(
        kern, grid=(b, n_tiles), in_specs=in_specs, out_specs=out_specs, out_shape=out_shape,
        scratch_shapes=[pltpu.VMEM((tm + 8, D_CONV), F32)],
        compiler_params=pltpu.CompilerParams(dimension_semantics=("arbitrary", "arbitrary"),
                                             vmem_limit_bytes=VMEM_LIMIT),
        name="inproj_prompt",
    )(x, mod, w_main, w_dt, cos, sin, conv_w, conv_b, dtb_pad)


def _ssd_kernel(xbc_ref, z_ref, dt_ref, dtT_ref, alog_ref, alogT_ref, dskip_ref, gssm_ref,
                y_ref, hlast_ref, state_ref, ybuf_ref, *, n_chunks):
    c = pl.program_id(1)

    @pl.when(c == 0)
    def _():
        state_ref[...] = jnp.zeros_like(state_ref)

    a_row = -jnp.exp(alog_ref[...])
    a_col = -jnp.exp(alogT_ref[...])
    dt = dt_ref[0]
    dtT = dtT_ref[0]
    row = lax.broadcasted_iota(jnp.int32, (BLK, BLK), 0)
    col = lax.broadcasted_iota(jnp.int32, (BLK, BLK), 1)
    tril = row >= col
    tri_l = tril.astype(BF16)
    tri_u = (row <= col).astype(BF16)
    lo = col < HEAD_DIM

    cum = sum(_dot(tri_l, p) for p in _split3(dt * a_row))
    cumT = sum(_dot(p, tri_u) for p in _split3(dtT * a_col))
    ecum = jnp.exp(cum)
    dte = jnp.exp(cum[BLK - 1:BLK, :] - cum) * dt
    ecl = jnp.exp(cumT[:, BLK - 1:BLK])

    hrow = lax.broadcasted_iota(jnp.int32, (N_HEADS, D_SSM), 0)
    hcol = lax.broadcasted_iota(jnp.int32, (N_HEADS, D_SSM), 1)
    sel_hd = (hcol // HEAD_DIM == hrow).astype(BF16)
    spread = lambda v: sum(_dot(p, sel_hd) for p in _split3(v)[:2])
    ecum_b = spread(ecum)
    dte_b = spread(dte)

    for g in range(SSM_GROUPS):
        bg = xbc_ref[0, :, D_SSM + g * D_STATE:D_SSM + (g + 1) * D_STATE]
        cg = xbc_ref[0, :, D_SSM + (SSM_GROUPS + g) * D_STATE:D_SSM + (SSM_GROUPS + g + 1) * D_STATE]
        cb = _dot_nt(cg, bg)
        for jj in range(2):
            j = 2 * g + jj
            h0, h1 = 2 * j, 2 * j + 1
            xp = xbc_ref[0, :, j * LANES:(j + 1) * LANES]
            y_pair = jnp.zeros((BLK, LANES), F32)
            for hm, h in ((lo, h0), (~lo, h1)):
                seg = cum[:, h:h + 1] - cumT[h:h + 1, :]
                dec = jnp.where(tril, jnp.exp(seg), 0.0)
                m = (cb * dec * dtT[h:h + 1, :]).astype(BF16)
                y_pair = y_pair + _dot(m, jnp.where(hm, xp, jnp.zeros_like(xp)))
            s_pair = state_ref[j]
            e_pair = ecum_b[:, j * LANES:(j + 1) * LANES]
            y_pair = y_pair + _dot_nt(cg, s_pair.astype(BF16)) * e_pair
            w_pair = dte_b[:, j * LANES:(j + 1) * LANES]
            xf = xp.astype(F32)
            upd = _dot_tn((xf * w_pair).astype(BF16), bg)
            scale = jnp.where(row < HEAD_DIM, ecl[h0:h0 + 1, :], ecl[h1:h1 + 1, :])
            state_ref[j] = s_pair * scale + upd
            sl = slice(j * LANES, (j + 1) * LANES)
            ybuf_ref[:, sl] = y_pair + dskip_ref[:, sl] * xf

    gw = D_SSM // SSM_GROUPS
    for g in range(SSM_GROUPS):
        sl = slice(g * gw, (g + 1) * gw)
        y = ybuf_ref[:, sl] * _silu(z_ref[0, :, sl].astype(F32))
        y_ref[0, :, sl] = (_rms(y) * gssm_ref[:, sl]).astype(BF16)

    @pl.when(c == n_chunks - 1)
    def _():
        hlast_ref[0] = state_ref[...]


def _ssd_prompt(xbc, z, dt, dtT, a_log, a_logT, dskip_lane, g_ssm):
    b, s, _ = xbc.shape
    n_chunks = s // BLK
    row = lambda bi, c: (bi, c, 0)
    return pl.pallas_call(
        functools.partial(_ssd_kernel, n_chunks=n_chunks),
        grid=(b, n_chunks),
        in_specs=[
            pl.BlockSpec((1, BLK, D_CONV), row),
            pl.BlockSpec((1, BLK, D_SSM), row),
            pl.BlockSpec((1, BLK, N_HEADS), row),
            pl.BlockSpec((1, N_HEADS, BLK), lambda bi, c: (bi, 0, c)),
            pl.BlockSpec((1, N_HEADS), lambda bi, c: (0, 0)),
            pl.BlockSpec((N_HEADS, 1), lambda bi, c: (0, 0)),
            pl.BlockSpec((1, D_SSM), lambda bi, c: (0, 0)),
            pl.BlockSpec((1, D_SSM), lambda bi, c: (0, 0)),
        ],
        out_specs=[
            pl.BlockSpec((1, BLK, D_SSM), row),
            pl.BlockSpec((1, N_HEADS // 2, LANES, D_STATE), lambda bi, c: (bi, 0, 0, 0)),
        ],
        out_shape=[
            jax.ShapeDtypeStruct((b, s, D_SSM), BF16),
            jax.ShapeDtypeStruct((b, N_HEADS // 2, LANES, D_STATE), F32),
        ],
        scratch_shapes=[pltpu.VMEM((N_HEADS // 2, LANES, D_STATE), F32), pltpu.VMEM((BLK, D_SSM), F32)],
        compiler_params=pltpu.CompilerParams(dimension_semantics=("arbitrary", "arbitrary"),
                                             vmem_limit_bytes=VMEM_LIMIT),
        name="ssd_prompt",
    )(xbc, z, dt, dtT, a_log, a_logT, dskip_lane, g_ssm)


SUPER = MAX_WINDOW
GROUP = 4
SUB = 4
TILE_BLOCKS = SUPER // BLK
N_PHASE = TILE_BLOCKS // GROUP


def _attn_stage(t, q_ref, kp_ref, kc_ref, vp_ref, vc_ref, qf, kf, vf):
    qf[...] = q_ref[0].astype(F32)
    kf[SUPER:, :] = kc_ref[0].astype(F32)
    vf[SUPER:, :] = vc_ref[0].astype(F32)

    @pl.when(t == 0)
    def _():
        kf[0:SUPER, :] = jnp.zeros((SUPER, LANES), F32)
        vf[0:SUPER, :] = jnp.zeros((SUPER, LANES), F32)

    @pl.when(t > 0)
    def _():
        kf[0:SUPER, :] = kp_ref[0].astype(F32)
        vf[0:SUPER, :] = vp_ref[0].astype(F32)


def _attn_group(ci, dil, g, t, qf, kf, vf, acc_s, lse_s):
    nblk = SUPER // (BLK * dil)
    qi = lax.broadcasted_iota(jnp.int32, (BLK, 2 * BLK), 0)
    kj = lax.broadcasted_iota(jnp.int32, (BLK, 2 * BLK), 1)
    band = (kj >= qi) & (kj <= qi + BLK)
    in_cur = kj >= BLK
    lo = lax.broadcasted_iota(jnp.int32, (BLK, LANES), 1) < HEAD_DIM
    lo_k = lax.broadcasted_iota(jnp.int32, (2 * BLK, LANES), 1) < HEAD_DIM
    stride = None if dil == 1 else dil
    for u0 in range(0, GROUP, SUB):
        blocks = []
        for u in range(u0, u0 + SUB):
            if nblk == GROUP:
                r, n = g, u
            elif nblk > GROUP:
                r, n = g // (nblk // GROUP), (g % (nblk // GROUP)) * GROUP + u
            else:
                r, n = g * (GROUP // nblk) + u // nblk, u % nblk
            qs = r + dil * BLK * n
            ks = SUPER + r + dil * BLK * (n - 1)
            qb = qf[pl.ds(qs, BLK, stride=stride), :].astype(BF16)
            kb = kf[pl.ds(ks, 2 * BLK, stride=stride), :].astype(BF16)
            vb = vf[pl.ds(ks, 2 * BLK, stride=stride), :].astype(BF16)
            mask = band & (in_cur | (t > 0) | (n > 0))
            blocks.append((qs, qb, kb, vb, mask))
        stats = []
        for qs, qb, kb, vb, mask in blocks:
            for hm in (lo, ~lo):
                s = jnp.where(mask, _dot_nt(jnp.where(hm, qb, jnp.zeros_like(qb)), kb), NEG_INF)
                stats.append((s, jnp.max(s, axis=1, keepdims=True)))
        probs = []
        ones = jnp.ones((2 * BLK, LANES), BF16)
        for s, m in stats:
            p = jnp.exp(s - m).astype(BF16)
            probs.append((p, _dot(p, ones)))
        for bi, (qs, qb, kb, vb, mask) in enumerate(blocks):
            (p0, l0), (p1, l1) = probs[2 * bi], probs[2 * bi + 1]
            m0, m1 = stats[2 * bi][1], stats[2 * bi + 1][1]
            zero = jnp.zeros_like(vb)
            acc = _dot(p0, jnp.where(lo_k, vb, zero)) + _dot(p1, jnp.where(lo_k, zero, vb))
            rows = pl.ds(qs, BLK, stride=stride)
            acc_s[ci, rows, :] = acc * jnp.where(lo, 1.0 / l0, 1.0 / l1)
            lse_s[ci, rows, :] = jnp.where(lo, m0 + jnp.log(l0), m1 + jnp.log(l1))


def _attn_merge(o_ref, acc_s, lse_s):
    n_cfg = len(DILATED_CONFIGS)
    rc = 256
    for c0 in range(0, SUPER, rc):
        rows = slice(c0, c0 + rc)
        ls = [lse_s[ci, rows, :] for ci in range(n_cfg)]
        mx = functools.reduce(jnp.maximum, ls)
        num = jnp.zeros((rc, LANES), F32)
        den = jnp.zeros((rc, LANES), F32)
        for ci in range(n_cfg):
            a = jnp.exp(ls[ci] - mx)
            num = num + a * acc_s[ci, rows, :]
            den = den + a
        o_ref[0, rows, :] = (num / den).astype(o_ref.dtype)


def _tail_kernel(x_ref, mod_ref, attn_ref, yssm_ref, gattn_ref, gfin_ref, wout_ref, wup_ref, wdown_ref, y_ref,
                 *, sample, tm):
    if sample:
        mod = lambda k: mod_ref[0:tm, k * D_MODEL:(k + 1) * D_MODEL]
        attn = attn_ref[...].T
    else:
        mod = lambda k: mod_ref[0, k:k + 1, :]
        attn = attn_ref[0].astype(F32)
    x = x_ref[...].reshape(tm, D_MODEL)
    attn_n = (_rms(attn) * gattn_ref[...]).astype(BF16)
    mix = _dot(attn_n, wout_ref[0:D_ATTN, :]) + _dot(yssm_ref[...].reshape(tm, D_SSM), wout_ref[D_ATTN:, :])
    x1 = x + mod(2) * mix
    h2 = (_rms(x1) * (1.0 + mod(4)) + mod(3)).astype(BF16)
    d_ff = wup_ref.shape[1]
    cw = 1024
    acc = jnp.zeros((tm, D_MODEL), F32)
    for cc in range(d_ff // cw):
        u = jnp.maximum(_dot(h2, wup_ref[:, cc * cw:(cc + 1) * cw]), 0.0)
        acc = acc + _dot((u * u).astype(BF16), wdown_ref[cc * cw:(cc + 1) * cw, :])
    x2 = x1 + mod(5) * acc
    y_ref[...] = (_rms(x2) * gfin_ref[...]).reshape(y_ref.shape)


def _tail_prompt(x, mod, attn, y_ssm, g_attn, g_final, w_out, w_up, w_down, tm=512):
    b, s, d = x.shape
    row = lambda bi, i: (bi, i, 0)
    in_specs = [pl.BlockSpec((1, tm, d), row), pl.BlockSpec((1, 6, d), lambda bi, i: (bi, 0, 0)),
                pl.BlockSpec((1, tm, D_ATTN), row), pl.BlockSpec((1, tm, D_SSM), row),
                _const_spec(g_attn.shape), _const_spec(g_final.shape),
                _const_spec(w_out.shape), _const_spec(w_up.shape), _const_spec(w_down.shape)]
    return pl.pallas_call(
        functools.partial(_tail_kernel, sample=False, tm=tm),
        grid=(b, s // tm), in_specs=in_specs,
        out_specs=pl.BlockSpec((1, tm, d), row),
        out_shape=jax.ShapeDtypeStruct((b, s, d), F32),
        compiler_params=pltpu.CompilerParams(dimension_semantics=("arbitrary", "arbitrary"),
                                             vmem_limit_bytes=VMEM_LIMIT),
        name="tail_prompt",
    )(x, mod, attn, y_ssm, g_attn, g_final, w_out, w_up, w_down)


def _tail_sample(x, mod_t, attn, y_ssm, g_attn, g_final, w_out, w_up, w_down):
    m, d = x.shape
    full = lambda shape: pl.BlockSpec(shape, lambda i: (0,) * len(shape))
    in_specs = [full((m, d)), full(mod_t.shape), full((D_ATTN, m)), full((m, D_SSM)),
                full(g_attn.shape), full(g_final.shape), full(w_out.shape), full(w_up.shape), full(w_down.shape)]
    return pl.pallas_call(
        functools.partial(_tail_kernel, sample=True, tm=m),
        grid=(1,), in_specs=in_specs,
        out_specs=full((m, d)),
        out_shape=jax.ShapeDtypeStruct((m, d), F32),
        compiler_params=pltpu.CompilerParams(dimension_semantics=("arbitrary",), vmem_limit_bytes=VMEM_LIMIT),
        name="tail_sample",
    )(x, mod_t, attn, y_ssm, g_attn, g_final, w_out, w_up, w_down)


def _inproj_sample_kernel(x_ref, mod_ref, w_ref, wdt_ref, cos_ref, sin_ref, convst_ref, convw_ref, convb_ref,
                          dtb_ref, qT_ref, kT_ref, vT_ref, z_ref, xbc_ref, xT_ref, dtT_ref, convout_ref):
    m = x_ref.shape[0]
    h = _rms(x_ref[...]) * (1.0 + mod_ref[0:m, D_MODEL:2 * D_MODEL]) + mod_ref[0:m, 0:D_MODEL]
    hb = h.astype(BF16)
    cos = cos_ref[...]
    sin = sin_ref[...]
    lane = lax.broadcasted_iota(jnp.int32, (m, LANES), 1)
    lower = (lane % HEAD_DIM) < (HEAD_DIM // 2)
    for c in range(D_ATTN // LANES):
        sl = slice(c * LANES, (c + 1) * LANES)
        qc = _rope_chunk(_dot(hb, w_ref[:, c * LANES:(c + 1) * LANES]), cos, sin, lower)
        qT_ref[sl, :] = (qc * (HEAD_DIM ** -0.5)).T
        kc = _rope_chunk(_dot(hb, w_ref[:, D_ATTN + c * LANES:D_ATTN + (c + 1) * LANES]), cos, sin, lower)
        kT_ref[sl, :] = kc.T
        vT_ref[sl, :] = _dot(hb, w_ref[:, 2 * D_ATTN + c * LANES:2 * D_ATTN + (c + 1) * LANES]).T
    z_ref[...] = _dot(hb, w_ref[:, 3 * D_ATTN:3 * D_ATTN + D_SSM])
    base = 3 * D_ATTN + D_SSM
    xbc = _dot(hb, w_ref[:, base:base + D_CONV])
    acc = convb_ref[...] + convw_ref[CONV_WIDTH - 1:CONV_WIDTH, :] * xbc
    for w in range(CONV_WIDTH - 1):
        acc = acc + convw_ref[w:w + 1, :] * convst_ref[w]
    xc = _silu(acc)
    xbc_ref[...] = xc
    xT_ref[...] = xc[:, :D_SSM].T
    for w in range(CONV_WIDTH - 2):
        convout_ref[w] = convst_ref[w + 1]
    convout_ref[CONV_WIDTH - 2] = xbc
    dt = _softplus(_dot(hb, wdt_ref[...]) + dtb_ref[...])
    dtT_ref[...] = dt.T[:N_HEADS, :]


def _inproj_sample(x, mod_t, w_main, w_dt, cos, sin, conv_state, conv_w, conv_b, dtb_pad):
    m, d = x.shape
    full = lambda shape: pl.BlockSpec(shape, lambda i: (0,) * len(shape))
    ins = (x, mod_t, w_main, w_dt, cos, sin, conv_state, conv_w, conv_b, dtb_pad)
    out_shape = [
        jax.ShapeDtypeStruct((D_ATTN, m), F32), jax.ShapeDtypeStruct((D_ATTN, m), F32),
        jax.ShapeDtypeStruct((D_ATTN, m), F32), jax.ShapeDtypeStruct((m, D_SSM), F32),
        jax.ShapeDtypeStruct((m, D_CONV), F32), jax.ShapeDtypeStruct((D_SSM, m), F32),
        jax.ShapeDtypeStruct((N_HEADS, m), F32), jax.ShapeDtypeStruct((CONV_WIDTH - 1, m, D_CONV), F32),
    ]
    return pl.pallas_call(
        _inproj_sample_kernel, grid=(1,),
        in_specs=[full(a.shape) for a in ins],
        out_specs=[full(o.shape) for o in out_shape],
        out_shape=out_shape,
        compiler_params=pltpu.CompilerParams(dimension_semantics=("arbitrary",), vmem_limit_bytes=VMEM_LIMIT),
        name="inproj_sample",
    )(*ins)


def _shift_heads(b, heads, qT_ref, knT_ref, vnT_ref, kc_ref, vc_ref, oT_ref, ko_ref, vo_ref, *, win, nb):
    sel = lax.broadcasted_iota(jnp.int32, (HEAD_DIM, nb), 1) == b
    pick = lambda ref, rs: jnp.sum(jnp.where(sel, ref[rs, :], 0.0), axis=1, keepdims=True)
    nt = win // LANES
    lane = lax.broadcasted_iota(jnp.int32, (HEAD_DIM, LANES), 1)
    keep = lane < LANES - 1
    lane1 = lax.broadcasted_iota(jnp.int32, (1, LANES), 1)
    n_cfg = float(len(DILATED_CONFIGS))

    def mult(c):
        w = lane1 + c * LANES
        tot = jnp.zeros((1, LANES), F32)
        for wd, dil in DILATED_CONFIGS:
            tot = tot + ((w >= win - wd) & ((win - w) % dil == 0)).astype(F32)
        return tot

    mults = [mult(c) for c in range(nt)]

    for h in heads:
        rs = slice(h * HEAD_DIM, (h + 1) * HEAD_DIM)
        qc = pick(qT_ref, rs)
        knc = pick(knT_ref, rs)
        vnc = pick(vnT_ref, rs)
        s_self = jnp.sum(qc * knc, axis=0, keepdims=True)
        s_tiles = [None] * nt
        nxt = jnp.broadcast_to(knc, (HEAD_DIM, LANES))
        for c in reversed(range(nt)):
            tl = slice(c * LANES, (c + 1) * LANES)
            kt = kc_ref[0, h, :, tl]
            s_tiles[c] = jnp.where(mults[c] > 0, jnp.sum(kt * qc, axis=0, keepdims=True), NEG_INF)
            r = pltpu.roll(kt, LANES - 1, 1)
            ko_ref[0, h, :, tl] = jnp.where(keep, r, nxt)
            nxt = r
        m = jnp.maximum(jnp.max(functools.reduce(jnp.maximum, s_tiles), axis=1, keepdims=True), s_self)
        p_self = n_cfg * jnp.exp(s_self - m)
        p_tiles = [mults[c] * jnp.exp(s_tiles[c] - m) for c in range(nt)]
        l = jnp.sum(functools.reduce(lambda a, b_: a + b_, p_tiles), axis=1, keepdims=True) + p_self
        acc = jnp.zeros((HEAD_DIM, LANES), F32)
        nxt = jnp.broadcast_to(vnc, (HEAD_DIM, LANES))
        for c in reversed(range(nt)):
            tl = slice(c * LANES, (c + 1) * LANES)
            vt = vc_ref[0, h, :, tl]
            acc = acc + vt * p_tiles[c]
            r = pltpu.roll(vt, LANES - 1, 1)
            vo_ref[0, h, :, tl] = jnp.where(keep, r, nxt)
            nxt = r
        o = jnp.sum(acc, axis=1, keepdims=True) + p_self * vnc
        oT_ref[rs, :] = jnp.where(sel, o / l, oT_ref[rs, :])


def _attn_shift_kernel(qT_ref, knT_ref, vnT_ref, kc_ref, vc_ref, q_ref, kp_ref, kcur_ref, vp_ref, vcur_ref,
                       oT_ref, ko_ref, vo_ref, o_ref, qf, kf, vf, acc_s, lse_s, *, hpb, win, nb, n_tiles):
    b = pl.program_id(1)
    step = pl.program_id(0) * nb + b
    phase = step % N_PHASE
    t = (step // N_PHASE) % n_tiles

    @pl.when(b == 0)
    def _():
        oT_ref[...] = jnp.zeros_like(oT_ref)

    @pl.when(phase == 0)
    def _():
        _attn_stage(t, q_ref, kp_ref, kcur_ref, vp_ref, vcur_ref, qf, kf, vf)

    n_cfg = len(DILATED_CONFIGS)
    for ci, (_, dil) in enumerate(DILATED_CONFIGS):
        _attn_group(ci, dil, phase, t, qf, kf, vf, acc_s, lse_s)
        heads = range(ci * hpb // n_cfg, (ci + 1) * hpb // n_cfg)
        _shift_heads(b, heads, qT_ref, knT_ref, vnT_ref, kc_ref, vc_ref, oT_ref, ko_ref, vo_ref, win=win, nb=nb)

    @pl.when(phase == N_PHASE - 1)
    def _():
        _attn_merge(o_ref, acc_s, lse_s)


def _attn_shift(qT, knT, vnT, cache_kT, cache_vT, q, k, v, hpb=8):
    nb, nh, hd, win = cache_kT.shape
    bp, s, d = q.shape
    n_tiles, n_pairs = s // SUPER, d // LANES
    n_steps = (nh // hpb) * nb
    assert bp * n_pairs * n_tiles * N_PHASE == n_steps, "attention tiles must fill the cache-shift grid exactly"
    n_cfg = len(DILATED_CONFIGS)
    colspec = pl.BlockSpec((hpb * hd, nb), lambda g, b: (g, 0))
    cspec = pl.BlockSpec((1, hpb, hd, win), lambda g, b: (b, g, 0, 0))

    def tile_index(g, b, prev):
        a = (g * nb + b) // N_PHASE
        t = a % n_tiles
        return (a // (n_pairs * n_tiles), jnp.maximum(t - 1, 0) if prev else t, (a // n_tiles) % n_pairs)

    cur = lambda g, b: tile_index(g, b, False)
    prev = lambda g, b: tile_index(g, b, True)
    blk = (1, SUPER, LANES)
    return pl.pallas_call(
        functools.partial(_attn_shift_kernel, hpb=hpb, win=win, nb=nb, n_tiles=n_tiles),
        grid=(nh // hpb, nb),
        in_specs=[colspec, colspec, colspec, cspec, cspec,
                  pl.BlockSpec(blk, cur), pl.BlockSpec(blk, prev), pl.BlockSpec(blk, cur),
                  pl.BlockSpec(blk, prev), pl.BlockSpec(blk, cur)],
        out_specs=[colspec, cspec, cspec, pl.BlockSpec(blk, cur)],
        out_shape=[jax.ShapeDtypeStruct((nh * hd, nb), F32),
                   jax.ShapeDtypeStruct(cache_kT.shape, F32), jax.ShapeDtypeStruct(cache_vT.shape, F32),
                   jax.ShapeDtypeStruct((bp, s, d), BF16)],
        scratch_shapes=[pltpu.VMEM((SUPER, LANES), F32), pltpu.VMEM((2 * SUPER, LANES), F32),
                        pltpu.VMEM((2 * SUPER, LANES), F32), pltpu.VMEM((n_cfg, SUPER, LANES), F32),
                        pltpu.VMEM((n_cfg, SUPER, LANES), F32)],
        compiler_params=pltpu.CompilerParams(dimension_semantics=("arbitrary", "arbitrary"),
                                             vmem_limit_bytes=VMEM_LIMIT),
        name="attn_shift",
    )(qT, knT, vnT, cache_kT, cache_vT, q, k, k, v, v)


def _ssm_sample_kernel(st_ref, xT_ref, b_ref, c_ref, dtT_ref, alogT_ref, new_ref, yT_ref, *, nb):
    h = pl.program_id(0)
    a_col = -jnp.exp(alogT_ref[...])
    dtT = dtT_ref[...]
    decT = jnp.exp(dtT * a_col)
    hsel = lax.broadcasted_iota(jnp.int32, (N_HEADS, nb), 0) == h
    dt_h = jnp.sum(jnp.where(hsel, dtT, 0.0), axis=0, keepdims=True)
    dec_h = jnp.sum(jnp.where(hsel, decT, 0.0), axis=0, keepdims=True)
    xdt = xT_ref[...] * dt_h
    lane = lax.broadcasted_iota(jnp.int32, (HEAD_DIM, nb), 1)
    y_acc = jnp.zeros((HEAD_DIM, nb), F32)
    rows_per_batch = 8
    for b0 in range(0, nb, rows_per_batch):
        prods = []
        for bi in range(b0, b0 + rows_per_batch):
            outer = xdt[:, bi:bi + 1] * b_ref[bi:bi + 1, :]
            new = st_ref[0, bi, 0] * dec_h[:, bi:bi + 1] + outer
            new_ref[0, bi, 0] = new
            prods.append(new * c_ref[bi:bi + 1, :])
        cols = [jnp.sum(p, axis=1, keepdims=True) for p in prods]
        for k, ycol in enumerate(cols):
            y_acc = jnp.where(lane == b0 + k, ycol, y_acc)
    yT_ref[...] = y_acc


def _ssm_sample(state, xT, xbc, dtT, a_logT):
    _, nb, nh, hp, n = state.shape
    hg = nh // SSM_GROUPS
    b_blk0 = D_SSM // D_STATE
    return pl.pallas_call(
        functools.partial(_ssm_sample_kernel, nb=nb),
        grid=(nh,),
        in_specs=[pl.BlockSpec((1, nb, 1, hp, n), lambda h: (0, 0, h, 0, 0)),
                  pl.BlockSpec((hp, nb), lambda h: (h, 0)),
                  pl.BlockSpec((nb, D_STATE), lambda h: (0, b_blk0 + h // hg)),
                  pl.BlockSpec((nb, D_STATE), lambda h: (0, b_blk0 + SSM_GROUPS + h // hg)),
                  pl.BlockSpec((nh, nb), lambda h: (0, 0)),
                  pl.BlockSpec((nh, 1), lambda h: (0, 0))],
        out_specs=[pl.BlockSpec((1, nb, 1, hp, n), lambda h: (0, 0, h, 0, 0)),
                   pl.BlockSpec((hp, nb), lambda h: (h, 0))],
        out_shape=[jax.ShapeDtypeStruct(state.shape, F32), jax.ShapeDtypeStruct((nh * hp, nb), F32)],
        compiler_params=pltpu.CompilerParams(dimension_semantics=("arbitrary",), vmem_limit_bytes=VMEM_LIMIT),
        name="ssm_sample",
    )(state, xT, xbc, xbc, dtT, a_logT)


def _ssm_gate_kernel(yT_ref, xs_ref, z_ref, dskip_ref, gssm_ref, o_ref):
    y = yT_ref[...].T + dskip_ref[...] * xs_ref[...]
    y = y * _silu(z_ref[...])
    gw = D_SSM // SSM_GROUPS
    for g in range(SSM_GROUPS):
        sl = slice(g * gw, (g + 1) * gw)
        o_ref[:, sl] = (_rms(y[:, sl]) * gssm_ref[:, sl]).astype(BF16)


def _ssm_gate(yT, xbc, z, dskip_lane, g_ssm):
    m = z.shape[0]
    full = lambda shape: pl.BlockSpec(shape, lambda i: (0,) * len(shape))
    return pl.pallas_call(
        _ssm_gate_kernel, grid=(1,),
        in_specs=[full(yT.shape), pl.BlockSpec((m, D_SSM), lambda i: (0, 0)), full(z.shape),
                  full(dskip_lane.shape), full(g_ssm.shape)],
        out_specs=full((m, D_SSM)),
        out_shape=jax.ShapeDtypeStruct((m, D_SSM), BF16),
        compiler_params=pltpu.CompilerParams(dimension_semantics=("arbitrary",), vmem_limit_bytes=VMEM_LIMIT),
        name="ssm_gate_sample",
    )(yT, xbc, z, dskip_lane, g_ssm)


def _rope_tables(pos):
    half = HEAD_DIM // 2
    inv_freq = ROPE_THETA ** (-jnp.arange(half, dtype=F32) / half)
    ang = pos.astype(F32)[:, None] * inv_freq[None, :]
    cos = jnp.tile(jnp.cos(ang), (1, LANES // half))
    sin = jnp.tile(jnp.sin(ang), (1, LANES // half))
    sign = jnp.where((jnp.arange(LANES) % HEAD_DIM) < half, -1.0, 1.0).astype(F32)
    return cos, sin * sign[None, :]


def kernel(x_prompt, x_sample, c_prompt, c_sample, cache_k_win, cache_v_win, state_conv, state_ssm, w_ada, b_ada,
           w_in, conv_w, conv_b, dt_bias, a_log, d_skip, g_attn, g_ssm, w_out, w_up, w_down, g_final):
    depth = w_ada.shape[0]
    assert depth == 1, "single-layer trunk"
    b, s, d = x_prompt.shape
    nb, t_new, _ = x_sample.shape
    assert t_new == 1
    win = cache_k_win.shape[2]
    assert win == MAX_WINDOW and s % MAX_WINDOW == 0
    past_len = PAST_LEN
    layer = 0

    w_in_l = w_in[layer]
    w_main = w_in_l.astype(BF16)
    w_dt = jnp.pad(w_in_l[:, D_MAIN:], ((0, 0), (0, LANES - N_HEADS))).astype(BF16)
    dtb_pad = jnp.pad(dt_bias[layer].astype(F32), (0, LANES - N_HEADS))[None, :]
    w_out_b = w_out[layer].astype(BF16)
    w_up_b = w_up[layer].astype(BF16)
    w_down_b = w_down[layer].astype(BF16)
    conv_w_l = conv_w[layer]
    conv_b_l = conv_b[layer][None, :]
    a_log_row = a_log[layer][None, :]
    a_log_col = a_log[layer][:, None]
    dskip_lane = jnp.repeat(d_skip[layer].astype(F32), HEAD_DIM)[None, :]
    g_attn_l = g_attn[layer][None, :]
    g_ssm_l = g_ssm[layer][None, :]
    g_final_l = g_final[None, :]

    pad_rows = (-(nb + b)) % 8
    c_all = jnp.concatenate([c_sample, c_prompt, jnp.zeros((pad_rows, d), F32)], axis=0)
    mod = _ada(c_all, w_ada[layer], b_ada[layer][None, :])
    mod_s = mod
    mod_p = mod[nb:nb + b].reshape(b, 6, d)

    cos_p, sin_p = _rope_tables(jnp.arange(s, dtype=jnp.int32))
    q, k, v, z, xbc, dt, dtT, k_win, v_win, conv_p = _inproj_prompt(
        x_prompt, mod_p, w_main, w_dt, cos_p, sin_p, conv_w_l, conv_b_l, dtb_pad)
    y_ssm, h_last = _ssd_prompt(xbc, z, dt, dtT, a_log_row, a_log_col, dskip_lane, g_ssm_l)

    cos_s, sin_s = _rope_tables(past_len + jnp.arange(1, dtype=jnp.int32))
    qT_s, kT_s, vT_s, z_s, xbc_s, xT_s, dtT_s, conv_s = _inproj_sample(
        x_sample.reshape(nb, d), mod_s, w_main, w_dt, cos_s, sin_s, state_conv[layer].transpose(1, 0, 2),
        conv_w_l, conv_b_l, dtb_pad)

    attnT_s, k_winT_s, v_winT_s, attn = _attn_shift(qT_s, kT_s, vT_s,
                                                    cache_k_win[layer].transpose(0, 2, 3, 1),
                                                    cache_v_win[layer].transpose(0, 2, 3, 1), q, k, v)
    y_prompt = _tail_prompt(x_prompt, mod_p, attn, y_ssm, g_attn_l, g_final_l, w_out_b, w_up_b, w_down_b)
    ssm_s, yT_s = _ssm_sample(state_ssm[layer][None], xT_s, xbc_s, dtT_s, a_log_col)
    y_ssm_s = _ssm_gate(yT_s, xbc_s, z_s, dskip_lane, g_ssm_l)
    y_sample = _tail_sample(x_sample.reshape(nb, d), mod_s, attnT_s, y_ssm_s,
                            g_attn_l, g_final_l, w_out_b, w_up_b, w_down_b)

    hd = (N_HEADS, HEAD_DIM)
    return (y_prompt, y_sample.reshape(nb, 1, d),
            k_win.reshape(1, b, -1, *hd), v_win.reshape(1, b, -1, *hd),
            conv_p[None], h_last.reshape(1, b, N_HEADS, HEAD_DIM, D_STATE),
            k_winT_s.transpose(0, 3, 1, 2)[None], v_winT_s.transpose(0, 3, 1, 2)[None],
            conv_s.transpose(1, 0, 2)[None], ssm_s)
```

```python
import functools
import math

import jax
import jax.numpy as jnp
from jax import lax
from jax.experimental import pallas as pl
from jax.experimental.pallas import tpu as pltpu

F32 = jnp.float32
BF16 = jnp.bfloat16

HEAD_DIM = 64
N_HEADS = 16
D_MODEL = 1024
D_ATTN = 1024
D_SSM = 1024
SSM_GROUPS = 4
D_STATE = 128
CONV_WIDTH = 4
D_CONV = D_SSM + 2 * SSM_GROUPS * D_STATE
D_MAIN = 3 * D_ATTN + D_SSM + D_CONV
DILATED_CONFIGS = ((128, 1), (512, 4), (2048, 16))
MAX_WINDOW = 2048
BLK = 128
ROPE_THETA = 10000.0
NEG_INF = -1e30
EPS = 1e-6
LANES = 128
VMEM_LIMIT = 56 * 1024 * 1024
PAST_LEN = 8192


def _dot(a, b):
    return jnp.dot(a, b, preferred_element_type=F32)


def _dot_nt(a, b):
    return lax.dot_general(a, b, (((1,), (1,)), ((), ())), preferred_element_type=F32)


def _dot_tn(a, b):
    return lax.dot_general(a, b, (((0,), (0,)), ((), ())), preferred_element_type=F32)


def _split3(x):
    hi = x.astype(BF16)
    r = x - hi.astype(F32)
    mid = r.astype(BF16)
    lo = (r - mid.astype(F32)).astype(BF16)
    return hi, mid, lo


def _rms(x):
    return x * lax.rsqrt(jnp.mean(x * x, axis=-1, keepdims=True) + EPS)


def _silu(x):
    return x / (1.0 + jnp.exp(-x))


def _softplus(x):
    return jnp.maximum(x, 0.0) + jnp.log1p(jnp.exp(-jnp.abs(x)))


def _rope_chunk(t, cos, sin_signed, lower):
    rot = jnp.where(lower, pltpu.roll(t, 96, 1), pltpu.roll(t, 32, 1))
    return t * cos + rot * sin_signed


def _const_spec(shape):
    n = len(shape)
    return pl.BlockSpec(shape, lambda *_: (0,) * n, pipeline_mode=pl.Buffered(1))


def _ada_kernel(c_ref, w_ref, b_ref, o_ref):
    c = _silu(c_ref[...]).astype(BF16)
    o_ref[...] = _dot(c, w_ref[...].astype(BF16)) + b_ref[...]


def _ada(c_all, w_ada, b_ada):
    m, d = c_all.shape
    n = w_ada.shape[1]
    tn = 1024
    return pl.pallas_call(
        _ada_kernel,
        grid=(n // tn,),
        in_specs=[pl.BlockSpec((m, d), lambda j: (0, 0)),
                  pl.BlockSpec((d, tn), lambda j: (0, j)),
                  pl.BlockSpec((1, tn), lambda j: (0, j))],
        out_specs=pl.BlockSpec((m, tn), lambda j: (0, j)),
        out_shape=jax.ShapeDtypeStruct((m, n), F32),
        compiler_params=pltpu.CompilerParams(dimension_semantics=("arbitrary",), vmem_limit_bytes=VMEM_LIMIT),
        name="ada_mod",
    )(c_all, w_ada, b_ada)


def _inproj_prompt_kernel(x_ref, mod_ref, w_ref, wdt_ref, cos_ref, sin_ref, convw_ref, convb_ref, dtb_ref,
                          q_ref, k_ref, v_ref, z_ref, xbc_ref, dt_ref, dtT_ref, kwin_ref, vwin_ref, convst_ref,
                          buf_ref, *, tm, n_tiles):
    i = pl.program_id(1)
    x = x_ref[0]
    h = _rms(x) * (1.0 + mod_ref[0, 1:2, :]) + mod_ref[0, 0:1, :]
    hb = h.astype(BF16)
    cos = cos_ref[...]
    sin = sin_ref[...]
    lane = lax.broadcasted_iota(jnp.int32, (tm, LANES), 1)
    lower = (lane % HEAD_DIM) < (HEAD_DIM // 2)

    @pl.when(i == 0)
    def _():
        buf_ref[0:8, :] = jnp.zeros((8, D_CONV), F32)

    cw = 256

    def rope_to(pair_ref, win_ref, scale, c0, r):
        for hc in range(cw // LANES):
            sl = slice(c0 + hc * LANES, c0 + (hc + 1) * LANES)
            t = _rope_chunk(r[:, hc * LANES:(hc + 1) * LANES], cos, sin, lower)
            if scale != 1.0:
                t = t * scale
            pair_ref[0, c0 // LANES + hc] = t.astype(BF16)
            if win_ref is not None:
                win_ref[0, :, sl] = t

    def ep_q(c0, r):
        rope_to(q_ref, None, HEAD_DIM ** -0.5, c0, r)

    def ep_k(c0, r):
        rope_to(k_ref, kwin_ref, 1.0, c0, r)

    def ep_v(c0, r):
        for hc in range(cw // LANES):
            v_ref[0, c0 // LANES + hc] = r[:, hc * LANES:(hc + 1) * LANES].astype(BF16)
        vwin_ref[0, :, c0:c0 + cw] = r

    def ep_z(c0, r):
        z_ref[0, :, c0:c0 + cw] = r.astype(BF16)

    def ep_xbc(c0, r):
        sl = slice(c0, c0 + cw)
        buf_ref[8:8 + tm, sl] = r
        acc = jnp.broadcast_to(convb_ref[:, sl], (tm, cw))
        for w in range(CONV_WIDTH):
            acc = acc + convw_ref[w:w + 1, sl] * buf_ref[5 + w:5 + w + tm, sl]
        xbc_ref[0, :, sl] = _silu(acc).astype(BF16)

    regions = ((0, D_ATTN, ep_q), (D_ATTN, D_ATTN, ep_k), (2 * D_ATTN, D_ATTN, ep_v),
               (3 * D_ATTN, D_SSM, ep_z), (3 * D_ATTN + D_SSM, D_CONV, ep_xbc))
    pending = None
    for base, width, ep in regions:
        for c0 in range(0, width, cw):
            r = _dot(hb, w_ref[:, base + c0:base + c0 + cw])
            if pending is not None:
                pending[0](pending[1], pending[2])
            pending = (ep, c0, r)
    dt_raw = _dot(hb, wdt_ref[...])
    pending[0](pending[1], pending[2])

    dt = _softplus(dt_raw + dtb_ref[...])
    dt_ref[0] = dt[:, :N_HEADS]
    dtT_ref[0] = dt.T[:N_HEADS, :]
    buf_ref[0:8, :] = buf_ref[tm:tm + 8, :]

    @pl.when(i == n_tiles - 1)
    def _():
        convst_ref[0] = buf_ref[tm + 5:tm + 8, :]


def _inproj_prompt(x, mod, w_main, w_dt, cos, sin, conv_w, conv_b, dtb_pad, tm=512):
    b, s, d = x.shape
    n_tiles = s // tm
    win = min(MAX_WINDOW, s)
    first_win_tile = (s - win) // tm
    row = lambda bi, i: (bi, i, 0)
    winrow = lambda bi, i: (bi, jnp.maximum(i - first_win_tile, 0), 0)
    kern = functools.partial(_inproj_prompt_kernel, tm=tm, n_tiles=n_tiles)
    out_shape = [
        jax.ShapeDtypeStruct((b, D_ATTN // LANES, s, LANES), BF16),
        jax.ShapeDtypeStruct((b, D_ATTN // LANES, s, LANES), BF16),
        jax.ShapeDtypeStruct((b, D_ATTN // LANES, s, LANES), BF16),
        jax.ShapeDtypeStruct((b, s, D_SSM), BF16),
        jax.ShapeDtypeStruct((b, s, D_CONV), BF16),
        jax.ShapeDtypeStruct((b, s, N_HEADS), F32),
        jax.ShapeDtypeStruct((b, N_HEADS, s), F32),
        jax.ShapeDtypeStruct((b, win, D_ATTN), F32),
        jax.ShapeDtypeStruct((b, win, D_ATTN), F32),
        jax.ShapeDtypeStruct((b, CONV_WIDTH - 1, D_CONV), F32),
    ]
    pair_rows = pl.BlockSpec((1, D_ATTN // LANES, tm, LANES), lambda bi, i: (bi, 0, i, 0))
    out_specs = [
        pair_rows,
        pair_rows,
        pair_rows,
        pl.BlockSpec((1, tm, D_SSM), row),
        pl.BlockSpec((1, tm, D_CONV), row),
        pl.BlockSpec((1, tm, N_HEADS), row),
        pl.BlockSpec((1, N_HEADS, tm), lambda bi, i: (bi, 0, i)),
        pl.BlockSpec((1, tm, D_ATTN), winrow),
        pl.BlockSpec((1, tm, D_ATTN), winrow),
        pl.BlockSpec((1, CONV_WIDTH - 1, D_CONV), lambda bi, i: (bi, 0, 0)),
    ]
    in_specs = [
        pl.BlockSpec((1, tm, d), row),
        pl.BlockSpec((1, 6, d), lambda bi, i: (bi, 0, 0)),
        _const_spec(w_main.shape),
        _const_spec(w_dt.shape),
        pl.BlockSpec((tm, LANES), lambda bi, i: (i, 0)),
        pl.BlockSpec((tm, LANES), lambda bi, i: (i, 0)),
        _const_spec(conv_w.shape),
        _const_spec(conv_b.shape),
        _const_spec(dtb_pad.shape),
    ]
    return pl.pallas_call(
        kern, grid=(b, n_tiles), in_specs=in_specs, out_specs=out_specs, out_shape=out_shape,
        scratch_shapes=[pltpu.VMEM((tm + 8, D_CONV), F32)],
        compiler_params=pltpu.CompilerParams(dimension_semantics=("arbitrary", "arbitrary"),
                                             vmem_limit_bytes=VMEM_LIMIT),
        name="inproj_prompt",
    )(x, mod, w_main, w_dt, cos, sin, conv_w, conv_b, dtb_pad)


def _ssd_kernel(xbc_ref, z_ref, dt_ref, dtT_ref, alog_ref, alogT_ref, dskip_ref, gssm_ref,
                y_ref, hlast_ref, state_ref, ybuf_ref, *, n_chunks):
    c = pl.program_id(1)

    @pl.when(c == 0)
    def _():
        state_ref[...] = jnp.zeros_like(state_ref)

    a_row = -jnp.exp(alog_ref[...])
    a_col = -jnp.exp(alogT_ref[...])
    dt = dt_ref[0]
    dtT = dtT_ref[0]
    row = lax.broadcasted_iota(jnp.int32, (BLK, BLK), 0)
    col = lax.broadcasted_iota(jnp.int32, (BLK, BLK), 1)
    tril = row >= col
    tri_l = tril.astype(BF16)
    tri_u = (row <= col).astype(BF16)
    lo = col < HEAD_DIM

    cum = sum(_dot(tri_l, p) for p in _split3(dt * a_row))
    cumT = sum(_dot(p, tri_u) for p in _split3(dtT * a_col))
    ecum = jnp.exp(cum)
    dte = jnp.exp(cum[BLK - 1:BLK, :] - cum) * dt
    ecl = jnp.exp(cumT[:, BLK - 1:BLK])

    hrow = lax.broadcasted_iota(jnp.int32, (N_HEADS, D_SSM), 0)
    hcol = lax.broadcasted_iota(jnp.int32, (N_HEADS, D_SSM), 1)
    sel_hd = (hcol // HEAD_DIM == hrow).astype(BF16)
    spread = lambda v: sum(_dot(p, sel_hd) for p in _split3(v)[:2])
    ecum_b = spread(ecum)
    dte_b = spread(dte)

    for g in range(SSM_GROUPS):
        bg = xbc_ref[0, :, D_SSM + g * D_STATE:D_SSM + (g + 1) * D_STATE]
        cg = xbc_ref[0, :, D_SSM + (SSM_GROUPS + g) * D_STATE:D_SSM + (SSM_GROUPS + g + 1) * D_STATE]
        cb = _dot_nt(cg, bg)
        for jj in range(2):
            j = 2 * g + jj
            h0, h1 = 2 * j, 2 * j + 1
            xp = xbc_ref[0, :, j * LANES:(j + 1) * LANES]
            y_pair = jnp.zeros((BLK, LANES), F32)
            for hm, h in ((lo, h0), (~lo, h1)):
                seg = cum[:, h:h + 1] - cumT[h:h + 1, :]
                dec = jnp.where(tril, jnp.exp(seg), 0.0)
                m = (cb * dec * dtT[h:h + 1, :]).astype(BF16)
                y_pair = y_pair + _dot(m, jnp.where(hm, xp, jnp.zeros_like(xp)))
            s_pair = state_ref[j]
            e_pair = ecum_b[:, j * LANES:(j + 1) * LANES]
            y_pair = y_pair + _dot_nt(cg, s_pair.astype(BF16)) * e_pair
            w_pair = dte_b[:, j * LANES:(j + 1) * LANES]
            xf = xp.astype(F32)
            upd = _dot_tn((xf * w_pair).astype(BF16), bg)
            scale = jnp.where(row < HEAD_DIM, ecl[h0:h0 + 1, :], ecl[h1:h1 + 1, :])
            state_ref[j] = s_pair * scale + upd
            sl = slice(j * LANES, (j + 1) * LANES)
            ybuf_ref[:, sl] = y_pair + dskip_ref[:, sl] * xf

    gw = D_SSM // SSM_GROUPS
    for g in range(SSM_GROUPS):
        sl = slice(g * gw, (g + 1) * gw)
        y = ybuf_ref[:, sl] * _silu(z_ref[0, :, sl].astype(F32))
        y_ref[0, :, sl] = (_rms(y) * gssm_ref[:, sl]).astype(BF16)

    @pl.when(c == n_chunks - 1)
    def _():
        hlast_ref[0] = state_ref[...]


def _ssd_prompt(xbc, z, dt, dtT, a_log, a_logT, dskip_lane, g_ssm):
    b, s, _ = xbc.shape
    n_chunks = s // BLK
    row = lambda bi, c: (bi, c, 0)
    return pl.pallas_call(
        functools.partial(_ssd_kernel, n_chunks=n_chunks),
        grid=(b, n_chunks),
        in_specs=[
            pl.BlockSpec((1, BLK, D_CONV), row),
            pl.BlockSpec((1, BLK, D_SSM), row),
            pl.BlockSpec((1, BLK, N_HEADS), row),
            pl.BlockSpec((1, N_HEADS, BLK), lambda bi, c: (bi, 0, c)),
            pl.BlockSpec((1, N_HEADS), lambda bi, c: (0, 0)),
            pl.BlockSpec((N_HEADS, 1), lambda bi, c: (0, 0)),
            pl.BlockSpec((1, D_SSM), lambda bi, c: (0, 0)),
            pl.BlockSpec((1, D_SSM), lambda bi, c: (0, 0)),
        ],
        out_specs=[
            pl.BlockSpec((1, BLK, D_SSM), row),
            pl.BlockSpec((1, N_HEADS // 2, LANES, D_STATE), lambda bi, c: (bi, 0, 0, 0)),
        ],
        out_shape=[
            jax.ShapeDtypeStruct((b, s, D_SSM), BF16),
            jax.ShapeDtypeStruct((b, N_HEADS // 2, LANES, D_STATE), F32),
        ],
        scratch_shapes=[pltpu.VMEM((N_HEADS // 2, LANES, D_STATE), F32), pltpu.VMEM((BLK, D_SSM), F32)],
        compiler_params=pltpu.CompilerParams(dimension_semantics=("arbitrary", "arbitrary"),
                                             vmem_limit_bytes=VMEM_LIMIT),
        name="ssd_prompt",
    )(xbc, z, dt, dtT, a_log, a_logT, dskip_lane, g_ssm)


SUPER = MAX_WINDOW
GROUP = 4
SUB = 4
TILE_BLOCKS = SUPER // BLK
N_PHASE = TILE_BLOCKS // GROUP


def _attn_stage(t, q_ref, kp_ref, kc_ref, vp_ref, vc_ref, qf, kf, vf):
    qf[...] = q_ref[0, 0].astype(F32)
    kf[SUPER:, :] = kc_ref[0, 0].astype(F32)
    vf[SUPER:, :] = vc_ref[0, 0].astype(F32)

    @pl.when(t == 0)
    def _():
        kf[0:SUPER, :] = jnp.zeros((SUPER, LANES), F32)
        vf[0:SUPER, :] = jnp.zeros((SUPER, LANES), F32)

    @pl.when(t > 0)
    def _():
        kf[0:SUPER, :] = kp_ref[0, 0].astype(F32)
        vf[0:SUPER, :] = vp_ref[0, 0].astype(F32)


def _attn_group(ci, dil, g, t, qf, kf, vf, acc_s, lse_s):
    nblk = SUPER // (BLK * dil)
    qi = lax.broadcasted_iota(jnp.int32, (BLK, 2 * BLK), 0)
    kj = lax.broadcasted_iota(jnp.int32, (BLK, 2 * BLK), 1)
    band = (kj >= qi) & (kj <= qi + BLK)
    in_cur = kj >= BLK
    lo = lax.broadcasted_iota(jnp.int32, (BLK, LANES), 1) < HEAD_DIM
    lo_k = lax.broadcasted_iota(jnp.int32, (2 * BLK, LANES), 1) < HEAD_DIM
    stride = None if dil == 1 else dil
    for u0 in range(0, GROUP, SUB):
        blocks = []
        for u in range(u0, u0 + SUB):
            if nblk == GROUP:
                r, n = g, u
            elif nblk > GROUP:
                r, n = g // (nblk // GROUP), (g % (nblk // GROUP)) * GROUP + u
            else:
                r, n = g * (GROUP // nblk) + u // nblk, u % nblk
            qs = r + dil * BLK * n
            ks = SUPER + r + dil * BLK * (n - 1)
            qb = qf[pl.ds(qs, BLK, stride=stride), :].astype(BF16)
            kb = kf[pl.ds(ks, 2 * BLK, stride=stride), :].astype(BF16)
            vb = vf[pl.ds(ks, 2 * BLK, stride=stride), :].astype(BF16)
            mask = band & (in_cur | (t > 0) | (n > 0))
            blocks.append((qs, qb, kb, vb, mask))
        stats = []
        for qs, qb, kb, vb, mask in blocks:
            for hm in (lo, ~lo):
                s = jnp.where(mask, _dot_nt(jnp.where(hm, qb, jnp.zeros_like(qb)), kb), NEG_INF)
                stats.append((s, jnp.max(s, axis=1, keepdims=True)))
        probs = []
        ones = jnp.ones((2 * BLK, LANES), BF16)
        for s, m in stats:
            p = jnp.exp(s - m).astype(BF16)
            probs.append((p, _dot(p, ones)))
        for bi, (qs, qb, kb, vb, mask) in enumerate(blocks):
            (p0, l0), (p1, l1) = probs[2 * bi], probs[2 * bi + 1]
            m0, m1 = stats[2 * bi][1], stats[2 * bi + 1][1]
            zero = jnp.zeros_like(vb)
            acc = _dot(p0, jnp.where(lo_k, vb, zero)) + _dot(p1, jnp.where(lo_k, zero, vb))
            rows = pl.ds(qs, BLK, stride=stride)
            acc_s[ci, rows, :] = acc * jnp.where(lo, 1.0 / l0, 1.0 / l1)
            lse_s[ci, rows, :] = jnp.where(lo, m0 + jnp.log(l0), m1 + jnp.log(l1))


def _attn_merge(o_ref, acc_s, lse_s):
    n_cfg = len(DILATED_CONFIGS)
    rc = 256
    for c0 in range(0, SUPER, rc):
        rows = slice(c0, c0 + rc)
        ls = [lse_s[ci, rows, :] for ci in range(n_cfg)]
        mx = functools.reduce(jnp.maximum, ls)
        num = jnp.zeros((rc, LANES), F32)
        den = jnp.zeros((rc, LANES), F32)
        for ci in range(n_cfg):
            a = jnp.exp(ls[ci] - mx)
            num = num + a * acc_s[ci, rows, :]
            den = den + a
        o_ref[0, 0, rows, :] = (num / den).astype(o_ref.dtype)


def _tail_kernel(x_ref, mod_ref, attn_ref, yssm_ref, gattn_ref, gfin_ref, wout_ref, wup_ref, wdown_ref, y_ref,
                 *, sample, tm):
    if sample:
        mod = lambda k: mod_ref[0:tm, k * D_MODEL:(k + 1) * D_MODEL]
        attn = attn_ref[...].T
    else:
        mod = lambda k: mod_ref[0, k:k + 1, :]
        attn = jnp.concatenate([attn_ref[0, c].astype(F32) for c in range(D_ATTN // LANES)], axis=1)
    x = x_ref[...].reshape(tm, D_MODEL)
    attn_n = (_rms(attn) * gattn_ref[...]).astype(BF16)
    mix = _dot(attn_n, wout_ref[0:D_ATTN, :]) + _dot(yssm_ref[...].reshape(tm, D_SSM), wout_ref[D_ATTN:, :])
    x1 = x + mod(2) * mix
    h2 = (_rms(x1) * (1.0 + mod(4)) + mod(3)).astype(BF16)
    d_ff = wup_ref.shape[1]
    cw = 1024
    acc = jnp.zeros((tm, D_MODEL), F32)
    for cc in range(d_ff // cw):
        u = jnp.maximum(_dot(h2, wup_ref[:, cc * cw:(cc + 1) * cw]), 0.0)
        acc = acc + _dot((u * u).astype(BF16), wdown_ref[cc * cw:(cc + 1) * cw, :])
    x2 = x1 + mod(5) * acc
    y_ref[...] = (_rms(x2) * gfin_ref[...]).reshape(y_ref.shape)


def _tail_prompt(x, mod, attn, y_ssm, g_attn, g_final, w_out, w_up, w_down, tm=512):
    b, s, d = x.shape
    row = lambda bi, i: (bi, i, 0)
    in_specs = [pl.BlockSpec((1, tm, d), row), pl.BlockSpec((1, 6, d), lambda bi, i: (bi, 0, 0)),
                pl.BlockSpec((1, D_ATTN // LANES, tm, LANES), lambda bi, i: (bi, 0, i, 0)),
                pl.BlockSpec((1, tm, D_SSM), row),
                _const_spec(g_attn.shape), _const_spec(g_final.shape),
                _const_spec(w_out.shape), _const_spec(w_up.shape), _const_spec(w_down.shape)]
    return pl.pallas_call(
        functools.partial(_tail_kernel, sample=False, tm=tm),
        grid=(b, s // tm), in_specs=in_specs,
        out_specs=pl.BlockSpec((1, tm, d), row),
        out_shape=jax.ShapeDtypeStruct((b, s, d), F32),
        compiler_params=pltpu.CompilerParams(dimension_semantics=("arbitrary", "arbitrary"),
                                             vmem_limit_bytes=VMEM_LIMIT),
        name="tail_prompt",
    )(x, mod, attn, y_ssm, g_attn, g_final, w_out, w_up, w_down)


def _tail_sample(x, mod_t, attn, y_ssm, g_attn, g_final, w_out, w_up, w_down):
    m, d = x.shape
    full = lambda shape: pl.BlockSpec(shape, lambda i: (0,) * len(shape))
    in_specs = [full((m, d)), full(mod_t.shape), full((D_ATTN, m)), full((m, D_SSM)),
                full(g_attn.shape), full(g_final.shape), full(w_out.shape), full(w_up.shape), full(w_down.shape)]
    return pl.pallas_call(
        functools.partial(_tail_kernel, sample=True, tm=m),
        grid=(1,), in_specs=in_specs,
        out_specs=full((m, d)),
        out_shape=jax.ShapeDtypeStruct((m, d), F32),
        compiler_params=pltpu.CompilerParams(dimension_semantics=("arbitrary",), vmem_limit_bytes=VMEM_LIMIT),
        name="tail_sample",
    )(x, mod_t, attn, y_ssm, g_attn, g_final, w_out, w_up, w_down)


def _inproj_sample_kernel(x_ref, mod_ref, w_ref, wdt_ref, cos_ref, sin_ref, convst_ref, convw_ref, convb_ref,
                          dtb_ref, qT_ref, kT_ref, vT_ref, z_ref, xbc_ref, xT_ref, dtT_ref, convout_ref):
    m = x_ref.shape[0]
    h = _rms(x_ref[...]) * (1.0 + mod_ref[0:m, D_MODEL:2 * D_MODEL]) + mod_ref[0:m, 0:D_MODEL]
    hb = h.astype(BF16)
    cos = cos_ref[...]
    sin = sin_ref[...]
    lane = lax.broadcasted_iota(jnp.int32, (m, LANES), 1)
    lower = (lane % HEAD_DIM) < (HEAD_DIM // 2)
    for c in range(D_ATTN // LANES):
        sl = slice(c * LANES, (c + 1) * LANES)
        qc = _rope_chunk(_dot(hb, w_ref[:, c * LANES:(c + 1) * LANES]), cos, sin, lower)
        qT_ref[sl, :] = (qc * (HEAD_DIM ** -0.5)).T
        kc = _rope_chunk(_dot(hb, w_ref[:, D_ATTN + c * LANES:D_ATTN + (c + 1) * LANES]), cos, sin, lower)
        kT_ref[sl, :] = kc.T
        vT_ref[sl, :] = _dot(hb, w_ref[:, 2 * D_ATTN + c * LANES:2 * D_ATTN + (c + 1) * LANES]).T
    z_ref[...] = _dot(hb, w_ref[:, 3 * D_ATTN:3 * D_ATTN + D_SSM])
    base = 3 * D_ATTN + D_SSM
    xbc = _dot(hb, w_ref[:, base:base + D_CONV])
    acc = convb_ref[...] + convw_ref[CONV_WIDTH - 1:CONV_WIDTH, :] * xbc
    for w in range(CONV_WIDTH - 1):
        acc = acc + convw_ref[w:w + 1, :] * convst_ref[w]
    xc = _silu(acc)
    xbc_ref[...] = xc
    xT_ref[...] = xc[:, :D_SSM].T
    for w in range(CONV_WIDTH - 2):
        convout_ref[w] = convst_ref[w + 1]
    convout_ref[CONV_WIDTH - 2] = xbc
    dt = _softplus(_dot(hb, wdt_ref[...]) + dtb_ref[...])
    dtT_ref[...] = dt.T[:N_HEADS, :]


def _inproj_sample(x, mod_t, w_main, w_dt, cos, sin, conv_state, conv_w, conv_b, dtb_pad):
    m, d = x.shape
    full = lambda shape: pl.BlockSpec(shape, lambda i: (0,) * len(shape))
    ins = (x, mod_t, w_main, w_dt, cos, sin, conv_state, conv_w, conv_b, dtb_pad)
    out_shape = [
        jax.ShapeDtypeStruct((D_ATTN, m), F32), jax.ShapeDtypeStruct((D_ATTN, m), F32),
        jax.ShapeDtypeStruct((D_ATTN, m), F32), jax.ShapeDtypeStruct((m, D_SSM), F32),
        jax.ShapeDtypeStruct((m, D_CONV), F32), jax.ShapeDtypeStruct((D_SSM, m), F32),
        jax.ShapeDtypeStruct((N_HEADS, m), F32), jax.ShapeDtypeStruct((CONV_WIDTH - 1, m, D_CONV), F32),
    ]
    return pl.pallas_call(
        _inproj_sample_kernel, grid=(1,),
        in_specs=[full(a.shape) for a in ins],
        out_specs=[full(o.shape) for o in out_shape],
        out_shape=out_shape,
        compiler_params=pltpu.CompilerParams(dimension_semantics=("arbitrary",), vmem_limit_bytes=VMEM_LIMIT),
        name="inproj_sample",
    )(*ins)


def _shift_heads(b, heads, qT_ref, knT_ref, vnT_ref, kc_ref, vc_ref, oT_ref, ko_ref, vo_ref, *, win, nb):
    sel = lax.broadcasted_iota(jnp.int32, (HEAD_DIM, nb), 1) == b
    pick = lambda ref, rs: jnp.sum(jnp.where(sel, ref[rs, :], 0.0), axis=1, keepdims=True)
    nt = win // LANES
    lane = lax.broadcasted_iota(jnp.int32, (HEAD_DIM, LANES), 1)
    keep = lane < LANES - 1
    lane1 = lax.broadcasted_iota(jnp.int32, (1, LANES), 1)
    n_cfg = float(len(DILATED_CONFIGS))

    def mult(c):
        w = lane1 + c * LANES
        tot = jnp.zeros((1, LANES), F32)
        for wd, dil in DILATED_CONFIGS:
            tot = tot + ((w >= win - wd) & ((win - w) % dil == 0)).astype(F32)
        return tot

    mults = [mult(c) for c in range(nt)]

    for h in heads:
        rs = slice(h * HEAD_DIM, (h + 1) * HEAD_DIM)
        qc = pick(qT_ref, rs)
        knc = pick(knT_ref, rs)
        vnc = pick(vnT_ref, rs)
        s_self = jnp.sum(qc * knc, axis=0, keepdims=True)
        s_tiles = [None] * nt
        nxt = jnp.broadcast_to(knc, (HEAD_DIM, LANES))
        for c in reversed(range(nt)):
            tl = slice(c * LANES, (c + 1) * LANES)
            kt = kc_ref[0, h, :, tl]
            s_tiles[c] = jnp.where(mults[c] > 0, jnp.sum(kt * qc, axis=0, keepdims=True), NEG_INF)
            r = pltpu.roll(kt, LANES - 1, 1)
            ko_ref[0, h, :, tl] = jnp.where(keep, r, nxt)
            nxt = r
        m = jnp.maximum(jnp.max(functools.reduce(jnp.maximum, s_tiles), axis=1, keepdims=True), s_self)
        p_self = n_cfg * jnp.exp(s_self - m)
        p_tiles = [mults[c] * jnp.exp(s_tiles[c] - m) for c in range(nt)]
        l = jnp.sum(functools.reduce(lambda a, b_: a + b_, p_tiles), axis=1, keepdims=True) + p_self
        acc = jnp.zeros((HEAD_DIM, LANES), F32)
        nxt = jnp.broadcast_to(vnc, (HEAD_DIM, LANES))
        for c in reversed(range(nt)):
            tl = slice(c * LANES, (c + 1) * LANES)
            vt = vc_ref[0, h, :, tl]
            acc = acc + vt * p_tiles[c]
            r = pltpu.roll(vt, LANES - 1, 1)
            vo_ref[0, h, :, tl] = jnp.where(keep, r, nxt)
            nxt = r
        o = jnp.sum(acc, axis=1, keepdims=True) + p_self * vnc
        oT_ref[rs, :] = jnp.where(sel, o / l, oT_ref[rs, :])


def _attn_shift_kernel(qT_ref, knT_ref, vnT_ref, kc_ref, vc_ref, q_ref, kp_ref, kcur_ref, vp_ref, vcur_ref,
                       oT_ref, ko_ref, vo_ref, o_ref, qf, kf, vf, acc_s, lse_s, *, hpb, win, nb, n_tiles):
    b = pl.program_id(1)
    step = pl.program_id(0) * nb + b
    phase = step % N_PHASE
    t = (step // N_PHASE) % n_tiles

    @pl.when(b == 0)
    def _():
        oT_ref[...] = jnp.zeros_like(oT_ref)

    @pl.when(phase == 0)
    def _():
        _attn_stage(t, q_ref, kp_ref, kcur_ref, vp_ref, vcur_ref, qf, kf, vf)

    n_cfg = len(DILATED_CONFIGS)
    for ci, (_, dil) in enumerate(DILATED_CONFIGS):
        _attn_group(ci, dil, phase, t, qf, kf, vf, acc_s, lse_s)
        heads = range(ci * hpb // n_cfg, (ci + 1) * hpb // n_cfg)
        _shift_heads(b, heads, qT_ref, knT_ref, vnT_ref, kc_ref, vc_ref, oT_ref, ko_ref, vo_ref, win=win, nb=nb)

    @pl.when(phase == N_PHASE - 1)
    def _():
        _attn_merge(o_ref, acc_s, lse_s)


def _attn_shift(qT, knT, vnT, cache_kT, cache_vT, q, k, v, hpb=8):
    nb, nh, hd, win = cache_kT.shape
    bp, n_pairs, s, _ = q.shape
    n_tiles = s // SUPER
    n_steps = (nh // hpb) * nb
    assert bp * n_pairs * n_tiles * N_PHASE == n_steps, "attention tiles must fill the cache-shift grid exactly"
    n_cfg = len(DILATED_CONFIGS)
    colspec = pl.BlockSpec((hpb * hd, nb), lambda g, b: (g, 0))
    cspec = pl.BlockSpec((1, hpb, hd, win), lambda g, b: (b, g, 0, 0))

    def tile_index(g, b, prev):
        a = (g * nb + b) // N_PHASE
        t = a % n_tiles
        return (a // (n_pairs * n_tiles), (a // n_tiles) % n_pairs, jnp.maximum(t - 1, 0) if prev else t, 0)

    cur = lambda g, b: tile_index(g, b, False)
    prev = lambda g, b: tile_index(g, b, True)
    blk = (1, 1, SUPER, LANES)
    return pl.pallas_call(
        functools.partial(_attn_shift_kernel, hpb=hpb, win=win, nb=nb, n_tiles=n_tiles),
        grid=(nh // hpb, nb),
        in_specs=[colspec, colspec, colspec, cspec, cspec,
                  pl.BlockSpec(blk, cur), pl.BlockSpec(blk, prev), pl.BlockSpec(blk, cur),
                  pl.BlockSpec(blk, prev), pl.BlockSpec(blk, cur)],
        out_specs=[colspec, cspec, cspec, pl.BlockSpec(blk, cur)],
        out_shape=[jax.ShapeDtypeStruct((nh * hd, nb), F32),
                   jax.ShapeDtypeStruct(cache_kT.shape, F32), jax.ShapeDtypeStruct(cache_vT.shape, F32),
                   jax.ShapeDtypeStruct(q.shape, BF16)],
        scratch_shapes=[pltpu.VMEM((SUPER, LANES), F32), pltpu.VMEM((2 * SUPER, LANES), F32),
                        pltpu.VMEM((2 * SUPER, LANES), F32), pltpu.VMEM((n_cfg, SUPER, LANES), F32),
                        pltpu.VMEM((n_cfg, SUPER, LANES), F32)],
        compiler_params=pltpu.CompilerParams(dimension_semantics=("arbitrary", "arbitrary"),
                                             vmem_limit_bytes=VMEM_LIMIT),
        name="attn_shift",
    )(qT, knT, vnT, cache_kT, cache_vT, q, k, k, v, v)


def _ssm_sample_kernel(st_ref, xT_ref, b_ref, c_ref, dtT_ref, alogT_ref, new_ref, yT_ref, *, nb):
    h = pl.program_id(0)
    a_col = -jnp.exp(alogT_ref[...])
    dtT = dtT_ref[...]
    decT = jnp.exp(dtT * a_col)
    hsel = lax.broadcasted_iota(jnp.int32, (N_HEADS, nb), 0) == h
    dt_h = jnp.sum(jnp.where(hsel, dtT, 0.0), axis=0, keepdims=True)
    dec_h = jnp.sum(jnp.where(hsel, decT, 0.0), axis=0, keepdims=True)
    xdt = xT_ref[...] * dt_h
    lane = lax.broadcasted_iota(jnp.int32, (HEAD_DIM, nb), 1)
    y_acc = jnp.zeros((HEAD_DIM, nb), F32)
    rows_per_batch = 8
    for b0 in range(0, nb, rows_per_batch):
        prods = []
        for bi in range(b0, b0 + rows_per_batch):
            outer = xdt[:, bi:bi + 1] * b_ref[bi:bi + 1, :]
            new = st_ref[0, bi, 0] * dec_h[:, bi:bi + 1] + outer
            new_ref[0, bi, 0] = new
            prods.append(new * c_ref[bi:bi + 1, :])
        cols = [jnp.sum(p, axis=1, keepdims=True) for p in prods]
        for k, ycol in enumerate(cols):
            y_acc = jnp.where(lane == b0 + k, ycol, y_acc)
    yT_ref[...] = y_acc


def _ssm_sample(state, xT, xbc, dtT, a_logT):
    _, nb, nh, hp, n = state.shape
    hg = nh // SSM_GROUPS
    b_blk0 = D_SSM // D_STATE
    return pl.pallas_call(
        functools.partial(_ssm_sample_kernel, nb=nb),
        grid=(nh,),
        in_specs=[pl.BlockSpec((1, nb, 1, hp, n), lambda h: (0, 0, h, 0, 0)),
                  pl.BlockSpec((hp, nb), lambda h: (h, 0)),
                  pl.BlockSpec((nb, D_STATE), lambda h: (0, b_blk0 + h // hg)),
                  pl.BlockSpec((nb, D_STATE), lambda h: (0, b_blk0 + SSM_GROUPS + h // hg)),
                  pl.BlockSpec((nh, nb), lambda h: (0, 0)),
                  pl.BlockSpec((nh, 1), lambda h: (0, 0))],
        out_specs=[pl.BlockSpec((1, nb, 1, hp, n), lambda h: (0, 0, h, 0, 0)),
                   pl.BlockSpec((hp, nb), lambda h: (h, 0))],
        out_shape=[jax.ShapeDtypeStruct(state.shape, F32), jax.ShapeDtypeStruct((nh * hp, nb), F32)],
        compiler_params=pltpu.CompilerParams(dimension_semantics=("arbitrary",), vmem_limit_bytes=VMEM_LIMIT),
        name="ssm_sample",
    )(state, xT, xbc, xbc, dtT, a_logT)


def _ssm_gate_kernel(yT_ref, xs_ref, z_ref, dskip_ref, gssm_ref, o_ref):
    y = yT_ref[...].T + dskip_ref[...] * xs_ref[...]
    y = y * _silu(z_ref[...])
    gw = D_SSM // SSM_GROUPS
    for g in range(SSM_GROUPS):
        sl = slice(g * gw, (g + 1) * gw)
        o_ref[:, sl] = (_rms(y[:, sl]) * gssm_ref[:, sl]).astype(BF16)


def _ssm_gate(yT, xbc, z, dskip_lane, g_ssm):
    m = z.shape[0]
    full = lambda shape: pl.BlockSpec(shape, lambda i: (0,) * len(shape))
    return pl.pallas_call(
        _ssm_gate_kernel, grid=(1,),
        in_specs=[full(yT.shape), pl.BlockSpec((m, D_SSM), lambda i: (0, 0)), full(z.shape),
                  full(dskip_lane.shape), full(g_ssm.shape)],
        out_specs=full((m, D_SSM)),
        out_shape=jax.ShapeDtypeStruct((m, D_SSM), BF16),
        compiler_params=pltpu.CompilerParams(dimension_semantics=("arbitrary",), vmem_limit_bytes=VMEM_LIMIT),
        name="ssm_gate_sample",
    )(yT, xbc, z, dskip_lane, g_ssm)


def _rope_tables(pos):
    half = HEAD_DIM // 2
    inv_freq = ROPE_THETA ** (-jnp.arange(half, dtype=F32) / half)
    ang = pos.astype(F32)[:, None] * inv_freq[None, :]
    cos = jnp.tile(jnp.cos(ang), (1, LANES // half))
    sin = jnp.tile(jnp.sin(ang), (1, LANES // half))
    sign = jnp.where((jnp.arange(LANES) % HEAD_DIM) < half, -1.0, 1.0).astype(F32)
    return cos, sin * sign[None, :]


def kernel(x_prompt, x_sample, c_prompt, c_sample, cache_k_win, cache_v_win, state_conv, state_ssm, w_ada, b_ada,
           w_in, conv_w, conv_b, dt_bias, a_log, d_skip, g_attn, g_ssm, w_out, w_up, w_down, g_final):
    depth = w_ada.shape[0]
    assert depth == 1, "single-layer trunk"
    b, s, d = x_prompt.shape
    nb, t_new, _ = x_sample.shape
    assert t_new == 1
    win = cache_k_win.shape[2]
    assert win == MAX_WINDOW and s % MAX_WINDOW == 0
    past_len = PAST_LEN
    layer = 0

    w_in_l = w_in[layer]
    w_main = w_in_l.astype(BF16)
    w_dt = jnp.pad(w_in_l[:, D_MAIN:], ((0, 0), (0, LANES - N_HEADS))).astype(BF16)
    dtb_pad = jnp.pad(dt_bias[layer].astype(F32), (0, LANES - N_HEADS))[None, :]
    w_out_b = w_out[layer].astype(BF16)
    w_up_b = w_up[layer].astype(BF16)
    w_down_b = w_down[layer].astype(BF16)
    conv_w_l = conv_w[layer]
    conv_b_l = conv_b[layer][None, :]
    a_log_row = a_log[layer][None, :]
    a_log_col = a_log[layer][:, None]
    dskip_lane = jnp.repeat(d_skip[layer].astype(F32), HEAD_DIM)[None, :]
    g_attn_l = g_attn[layer][None, :]
    g_ssm_l = g_ssm[layer][None, :]
    g_final_l = g_final[None, :]

    pad_rows = (-(nb + b)) % 8
    c_all = jnp.concatenate([c_sample, c_prompt, jnp.zeros((pad_rows, d), F32)], axis=0)
    mod = _ada(c_all, w_ada[layer], b_ada[layer][None, :])
    mod_s = mod
    mod_p = mod[nb:nb + b].reshape(b, 6, d)

    cos_p, sin_p = _rope_tables(jnp.arange(s, dtype=jnp.int32))
    q, k, v, z, xbc, dt, dtT, k_win, v_win, conv_p = _inproj_prompt(
        x_prompt, mod_p, w_main, w_dt, cos_p, sin_p, conv_w_l, conv_b_l, dtb_pad)
    y_ssm, h_last = _ssd_prompt(xbc, z, dt, dtT, a_log_row, a_log_col, dskip_lane, g_ssm_l)

    cos_s, sin_s = _rope_tables(past_len + jnp.arange(1, dtype=jnp.int32))
    qT_s, kT_s, vT_s, z_s, xbc_s, xT_s, dtT_s, conv_s = _inproj_sample(
        x_sample.reshape(nb, d), mod_s, w_main, w_dt, cos_s, sin_s, state_conv[layer].transpose(1, 0, 2),
        conv_w_l, conv_b_l, dtb_pad)

    attnT_s, k_winT_s, v_winT_s, attn = _attn_shift(qT_s, kT_s, vT_s,
                                                    cache_k_win[layer].transpose(0, 2, 3, 1),
                                                    cache_v_win[layer].transpose(0, 2, 3, 1), q, k, v)
    y_prompt = _tail_prompt(x_prompt, mod_p, attn, y_ssm, g_attn_l, g_final_l, w_out_b, w_up_b, w_down_b)
    ssm_s, yT_s = _ssm_sample(state_ssm[layer][None], xT_s, xbc_s, dtT_s, a_log_col)
    y_ssm_s = _ssm_gate(yT_s, xbc_s, z_s, dskip_lane, g_ssm_l)
    y_sample = _tail_sample(x_sample.reshape(nb, d), mod_s, attnT_s, y_ssm_s,
                            g_attn_l, g_final_l, w_out_b, w_up_b, w_down_b)

    hd = (N_HEADS, HEAD_DIM)
    return (y_prompt, y_sample.reshape(nb, 1, d),
            k_win.reshape(1, b, -1, *hd), v_win.reshape(1, b, -1, *hd),
            conv_p[None], h_last.reshape(1, b, N_HEADS, HEAD_DIM, D_STATE),
            k_winT_s.transpose(0, 3, 1, 2)[None], v_winT_s.transpose(0, 3, 1, 2)[None],
            conv_s.transpose(1, 0, 2)[None], ssm_s)
```

```python
import functools
import math

import jax
import jax.numpy as jnp
from jax import lax
from jax.experimental import pallas as pl
from jax.experimental.pallas import tpu as pltpu

F32 = jnp.float32
BF16 = jnp.bfloat16

HEAD_DIM = 64
N_HEADS = 16
D_MODEL = 1024
D_ATTN = 1024
D_SSM = 1024
SSM_GROUPS = 4
D_STATE = 128
CONV_WIDTH = 4
D_CONV = D_SSM + 2 * SSM_GROUPS * D_STATE
D_MAIN = 3 * D_ATTN + D_SSM + D_CONV
DILATED_CONFIGS = ((128, 1), (512, 4), (2048, 16))
MAX_WINDOW = 2048
BLK = 128
ROPE_THETA = 10000.0
NEG_INF = -1e30
EPS = 1e-6
LANES = 128
VMEM_LIMIT = 56 * 1024 * 1024
PAST_LEN = 8192


def _dot(a, b):
    return jnp.dot(a, b, preferred_element_type=F32)


def _dot_nt(a, b):
    return lax.dot_general(a, b, (((1,), (1,)), ((), ())), preferred_element_type=F32)


def _dot_tn(a, b):
    return lax.dot_general(a, b, (((0,), (0,)), ((), ())), preferred_element_type=F32)


def _split3(x):
    hi = x.astype(BF16)
    r = x - hi.astype(F32)
    mid = r.astype(BF16)
    lo = (r - mid.astype(F32)).astype(BF16)
    return hi, mid, lo


def _rms(x):
    return x * lax.rsqrt(jnp.mean(x * x, axis=-1, keepdims=True) + EPS)


def _silu(x):
    return x / (1.0 + jnp.exp(-x))


def _softplus(x):
    return jnp.maximum(x, 0.0) + jnp.log1p(jnp.exp(-jnp.abs(x)))


def _rope_chunk(t, cos, sin_signed, lower):
    rot = jnp.where(lower, pltpu.roll(t, 96, 1), pltpu.roll(t, 32, 1))
    return t * cos + rot * sin_signed


def _const_spec(shape):
    n = len(shape)
    return pl.BlockSpec(shape, lambda *_: (0,) * n, pipeline_mode=pl.Buffered(1))


def _ada_kernel(c_ref, w_ref, b_ref, o_ref):
    c = _silu(c_ref[...]).astype(BF16)
    o_ref[...] = _dot(c, w_ref[...].astype(BF16)) + b_ref[...]


def _ada(c_all, w_ada, b_ada):
    m, d = c_all.shape
    n = w_ada.shape[1]
    tn = 1024
    return pl.pallas_call(
        _ada_kernel,
        grid=(n // tn,),
        in_specs=[pl.BlockSpec((m, d), lambda j: (0, 0)),
                  pl.BlockSpec((d, tn), lambda j: (0, j)),
                  pl.BlockSpec((1, tn), lambda j: (0, j))],
        out_specs=pl.BlockSpec((m, tn), lambda j: (0, j)),
        out_shape=jax.ShapeDtypeStruct((m, n), F32),
        compiler_params=pltpu.CompilerParams(dimension_semantics=("arbitrary",), vmem_limit_bytes=VMEM_LIMIT),
        name="ada_mod",
    )(c_all, w_ada, b_ada)


def _inproj_prompt_kernel(x_ref, mod_ref, w_ref, wdt_ref, cos_ref, sin_ref, convw_ref, convb_ref, dtb_ref,
                          q_ref, k_ref, v_ref, z_ref, xbc_ref, dt_ref, dtT_ref, kwin_ref, vwin_ref, convst_ref,
                          buf_ref, *, tm, n_tiles):
    i = pl.program_id(1)
    x = x_ref[0]
    h = _rms(x) * (1.0 + mod_ref[0, 1:2, :]) + mod_ref[0, 0:1, :]
    hb = h.astype(BF16)
    cos = cos_ref[...]
    sin = sin_ref[...]
    lane = lax.broadcasted_iota(jnp.int32, (tm, LANES), 1)
    lower = (lane % HEAD_DIM) < (HEAD_DIM // 2)

    @pl.when(i == 0)
    def _():
        buf_ref[0:8, :] = jnp.zeros((8, D_CONV), F32)

    cw = 256

    def rope_to(refs, scale, c0, r):
        for hc in range(cw // LANES):
            sl = slice(c0 + hc * LANES, c0 + (hc + 1) * LANES)
            t = _rope_chunk(r[:, hc * LANES:(hc + 1) * LANES], cos, sin, lower)
            if scale != 1.0:
                t = t * scale
            refs[0][0, :, sl] = t.astype(BF16)
            for extra in refs[1:]:
                extra[0, :, sl] = t

    def ep_q(c0, r):
        rope_to((q_ref,), HEAD_DIM ** -0.5, c0, r)

    def ep_k(c0, r):
        rope_to((k_ref, kwin_ref), 1.0, c0, r)

    def ep_v(c0, r):
        v_ref[0, :, c0:c0 + cw] = r.astype(BF16)
        vwin_ref[0, :, c0:c0 + cw] = r

    def ep_z(c0, r):
        z_ref[0, :, c0:c0 + cw] = r.astype(BF16)

    def ep_xbc(c0, r):
        sl = slice(c0, c0 + cw)
        buf_ref[8:8 + tm, sl] = r
        ext = buf_ref[0:tm + 8, sl]
        acc = convb_ref[:, sl] + convw_ref[CONV_WIDTH - 1:CONV_WIDTH, sl] * r
        for w in range(CONV_WIDTH - 1):
            acc = acc + convw_ref[w:w + 1, sl] * pltpu.roll(ext, CONV_WIDTH - 1 - w, 0)[8:8 + tm]
        xbc_ref[0, :, sl] = _silu(acc).astype(BF16)

    regions = ((0, D_ATTN, ep_q), (D_ATTN, D_ATTN, ep_k), (2 * D_ATTN, D_ATTN, ep_v),
               (3 * D_ATTN, D_SSM, ep_z), (3 * D_ATTN + D_SSM, D_CONV, ep_xbc))
    pending = None
    for base, width, ep in regions:
        for c0 in range(0, width, cw):
            r = _dot(hb, w_ref[:, base + c0:base + c0 + cw])
            if pending is not None:
                pending[0](pending[1], pending[2])
            pending = (ep, c0, r)
    dt_raw = _dot(hb, wdt_ref[...])
    pending[0](pending[1], pending[2])

    dt = _softplus(dt_raw + dtb_ref[...])
    dt_ref[0] = dt[:, :N_HEADS]
    dtT_ref[0] = dt.T[:N_HEADS, :]
    buf_ref[0:8, :] = buf_ref[tm:tm + 8, :]

    @pl.when(i == n_tiles - 1)
    def _():
        convst_ref[0] = buf_ref[tm + 5:tm + 8, :]


def _inproj_prompt(x, mod, w_main, w_dt, cos, sin, conv_w, conv_b, dtb_pad, tm=512):
    b, s, d = x.shape
    n_tiles = s // tm
    win = min(MAX_WINDOW, s)
    first_win_tile = (s - win) // tm
    row = lambda bi, i: (bi, i, 0)
    winrow = lambda bi, i: (bi, jnp.maximum(i - first_win_tile, 0), 0)
    kern = functools.partial(_inproj_prompt_kernel, tm=tm, n_tiles=n_tiles)
    out_shape = [
        jax.ShapeDtypeStruct((b, s, D_ATTN), BF16),
        jax.ShapeDtypeStruct((b, s, D_ATTN), BF16),
        jax.ShapeDtypeStruct((b, s, D_ATTN), BF16),
        jax.ShapeDtypeStruct((b, s, D_SSM), BF16),
        jax.ShapeDtypeStruct((b, s, D_CONV), BF16),
        jax.ShapeDtypeStruct((b, s, N_HEADS), F32),
        jax.ShapeDtypeStruct((b, N_HEADS, s), F32),
        jax.ShapeDtypeStruct((b, win, D_ATTN), F32),
        jax.ShapeDtypeStruct((b, win, D_ATTN), F32),
        jax.ShapeDtypeStruct((b, CONV_WIDTH - 1, D_CONV), F32),
    ]
    out_specs = [
        pl.BlockSpec((1, tm, D_ATTN), row),
        pl.BlockSpec((1, tm, D_ATTN), row),
        pl.BlockSpec((1, tm, D_ATTN), row),
        pl.BlockSpec((1, tm, D_SSM), row),
        pl.BlockSpec((1, tm, D_CONV), row),
        pl.BlockSpec((1, tm, N_HEADS), row),
        pl.BlockSpec((1, N_HEADS, tm), lambda bi, i: (bi, 0, i)),
        pl.BlockSpec((1, tm, D_ATTN), winrow),
        pl.BlockSpec((1, tm, D_ATTN), winrow),
        pl.BlockSpec((1, CONV_WIDTH - 1, D_CONV), lambda bi, i: (bi, 0, 0)),
    ]
    in_specs = [
        pl.BlockSpec((1, tm, d), row),
        pl.BlockSpec((1, 6, d), lambda bi, i: (bi, 0, 0)),
        _const_spec(w_main.shape),
        _const_spec(w_dt.shape),
        pl.BlockSpec((tm, LANES), lambda bi, i: (i, 0)),
        pl.BlockSpec((tm, LANES), lambda bi, i: (i, 0)),
        _const_spec(conv_w.shape),
        _const_spec(conv_b.shape),
        _const_spec(dtb_pad.shape),
    ]
    return pl.pallas_call(
        kern, grid=(b, n_tiles), in_specs=in_specs, out_specs=out_specs, out_shape=out_shape,
        scratch_shapes=[pltpu.VMEM((tm + 8, D_CONV), F32)],
        compiler_params=pltpu.CompilerParams(dimension_semantics=("arbitrary", "arbitrary"),
                                             vmem_limit_bytes=VMEM_LIMIT),
        name="inproj_prompt",
    )(x, mod, w_main, w_dt, cos, sin, conv_w, conv_b, dtb_pad)


def _ssd_kernel(xbc_ref, z_ref, dt_ref, dtT_ref, alog_ref, alogT_ref, dskip_ref, gssm_ref,
                y_ref, hlast_ref, state_ref, ybuf_ref, *, n_chunks):
    c = pl.program_id(1)

    @pl.when(c == 0)
    def _():
        state_ref[...] = jnp.zeros_like(state_ref)

    a_row = -jnp.exp(alog_ref[...])
    a_col = -jnp.exp(alogT_ref[...])
    dt = dt_ref[0]
    dtT = dtT_ref[0]
    row = lax.broadcasted_iota(jnp.int32, (BLK, BLK), 0)
    col = lax.broadcasted_iota(jnp.int32, (BLK, BLK), 1)
    tril = row >= col
    tri_l = tril.astype(BF16)
    tri_u = (row <= col).astype(BF16)
    lo = col < HEAD_DIM

    cum = sum(_dot(tri_l, p) for p in _split3(dt * a_row))
    cumT = sum(_dot(p, tri_u) for p in _split3(dtT * a_col))
    ecum = jnp.exp(cum)
    dte = jnp.exp(cum[BLK - 1:BLK, :] - cum) * dt
    ecl = jnp.exp(cumT[:, BLK - 1:BLK])

    hrow = lax.broadcasted_iota(jnp.int32, (N_HEADS, D_SSM), 0)
    hcol = lax.broadcasted_iota(jnp.int32, (N_HEADS, D_SSM), 1)
    sel_hd = (hcol // HEAD_DIM == hrow).astype(BF16)
    spread = lambda v: sum(_dot(p, sel_hd) for p in _split3(v)[:2])
    ecum_b = spread(ecum)
    dte_b = spread(dte)

    for g in range(SSM_GROUPS):
        bg = xbc_ref[0, :, D_SSM + g * D_STATE:D_SSM + (g + 1) * D_STATE]
        cg = xbc_ref[0, :, D_SSM + (SSM_GROUPS + g) * D_STATE:D_SSM + (SSM_GROUPS + g + 1) * D_STATE]
        cb = _dot_nt(cg, bg)
        for jj in range(2):
            j = 2 * g + jj
            h0, h1 = 2 * j, 2 * j + 1
            xp = xbc_ref[0, :, j * LANES:(j + 1) * LANES]
            y_pair = jnp.zeros((BLK, LANES), F32)
            for hm, h in ((lo, h0), (~lo, h1)):
                seg = cum[:, h:h + 1] - cumT[h:h + 1, :]
                dec = jnp.where(tril, jnp.exp(seg), 0.0)
                m = (cb * dec * dtT[h:h + 1, :]).astype(BF16)
                y_pair = y_pair + _dot(m, jnp.where(hm, xp, jnp.zeros_like(xp)))
            s_pair = state_ref[j]
            e_pair = ecum_b[:, j * LANES:(j + 1) * LANES]
            y_pair = y_pair + _dot_nt(cg, s_pair.astype(BF16)) * e_pair
            w_pair = dte_b[:, j * LANES:(j + 1) * LANES]
            xf = xp.astype(F32)
            upd = _dot_tn((xf * w_pair).astype(BF16), bg)
            scale = jnp.where(row < HEAD_DIM, ecl[h0:h0 + 1, :], ecl[h1:h1 + 1, :])
            state_ref[j] = s_pair * scale + upd
            sl = slice(j * LANES, (j + 1) * LANES)
            ybuf_ref[:, sl] = y_pair + dskip_ref[:, sl] * xf

    gw = D_SSM // SSM_GROUPS
    for g in range(SSM_GROUPS):
        sl = slice(g * gw, (g + 1) * gw)
        y = ybuf_ref[:, sl] * _silu(z_ref[0, :, sl].astype(F32))
        y_ref[0, :, sl] = (_rms(y) * gssm_ref[:, sl]).astype(BF16)

    @pl.when(c == n_chunks - 1)
    def _():
        hlast_ref[0] = state_ref[...]


def _ssd_prompt(xbc, z, dt, dtT, a_log, a_logT, dskip_lane, g_ssm):
    b, s, _ = xbc.shape
    n_chunks = s // BLK
    row = lambda bi, c: (bi, c, 0)
    return pl.pallas_call(
        functools.partial(_ssd_kernel, n_chunks=n_chunks),
        grid=(b, n_chunks),
        in_specs=[
            pl.BlockSpec((1, BLK, D_CONV), row),
            pl.BlockSpec((1, BLK, D_SSM), row),
            pl.BlockSpec((1, BLK, N_HEADS), row),
            pl.BlockSpec((1, N_HEADS, BLK), lambda bi, c: (bi, 0, c)),
            pl.BlockSpec((1, N_HEADS), lambda bi, c: (0, 0)),
            pl.BlockSpec((N_HEADS, 1), lambda bi, c: (0, 0)),
            pl.BlockSpec((1, D_SSM), lambda bi, c: (0, 0)),
            pl.BlockSpec((1, D_SSM), lambda bi, c: (0, 0)),
        ],
        out_specs=[
            pl.BlockSpec((1, BLK, D_SSM), row),
            pl.BlockSpec((1, N_HEADS // 2, LANES, D_STATE), lambda bi, c: (bi, 0, 0, 0)),
        ],
        out_shape=[
            jax.ShapeDtypeStruct((b, s, D_SSM), BF16),
            jax.ShapeDtypeStruct((b, N_HEADS // 2, LANES, D_STATE), F32),
        ],
        scratch_shapes=[pltpu.VMEM((N_HEADS // 2, LANES, D_STATE), F32), pltpu.VMEM((BLK, D_SSM), F32)],
        compiler_params=pltpu.CompilerParams(dimension_semantics=("arbitrary", "arbitrary"),
                                             vmem_limit_bytes=VMEM_LIMIT),
        name="ssd_prompt",
    )(xbc, z, dt, dtT, a_log, a_logT, dskip_lane, g_ssm)


SUPER = MAX_WINDOW
GROUP = 4
SUB = 4
TILE_BLOCKS = SUPER // BLK
N_PHASE = TILE_BLOCKS // GROUP


def _attn_stage(t, q_ref, kp_ref, kc_ref, vp_ref, vc_ref, qf, kf, vf):
    qf[...] = q_ref[0].astype(F32)
    kf[SUPER:, :] = kc_ref[0].astype(F32)
    vf[SUPER:, :] = vc_ref[0].astype(F32)

    @pl.when(t == 0)
    def _():
        kf[0:SUPER, :] = jnp.zeros((SUPER, LANES), F32)
        vf[0:SUPER, :] = jnp.zeros((SUPER, LANES), F32)

    @pl.when(t > 0)
    def _():
        kf[0:SUPER, :] = kp_ref[0].astype(F32)
        vf[0:SUPER, :] = vp_ref[0].astype(F32)


def _attn_group(ci, dil, g, t, qf, kf, vf, acc_s, lse_s):
    nblk = SUPER // (BLK * dil)
    qi = lax.broadcasted_iota(jnp.int32, (BLK, 2 * BLK), 0)
    kj = lax.broadcasted_iota(jnp.int32, (BLK, 2 * BLK), 1)
    band = (kj >= qi) & (kj <= qi + BLK)
    in_cur = kj >= BLK
    lo = lax.broadcasted_iota(jnp.int32, (BLK, LANES), 1) < HEAD_DIM
    lo_k = lax.broadcasted_iota(jnp.int32, (2 * BLK, LANES), 1) < HEAD_DIM
    stride = None if dil == 1 else dil
    for u0 in range(0, GROUP, SUB):
        blocks = []
        for u in range(u0, u0 + SUB):
            if nblk == GROUP:
                r, n = g, u
            elif nblk > GROUP:
                r, n = g // (nblk // GROUP), (g % (nblk // GROUP)) * GROUP + u
            else:
                r, n = g * (GROUP // nblk) + u // nblk, u % nblk
            qs = r + dil * BLK * n
            ks = SUPER + r + dil * BLK * (n - 1)
            qb = qf[pl.ds(qs, BLK, stride=stride), :].astype(BF16)
            kb = kf[pl.ds(ks, 2 * BLK, stride=stride), :].astype(BF16)
            vb = vf[pl.ds(ks, 2 * BLK, stride=stride), :].astype(BF16)
            mask = band & (in_cur | (t > 0) | (n > 0))
            blocks.append((qs, qb, kb, vb, mask))
        stats = []
        for qs, qb, kb, vb, mask in blocks:
            for hm in (lo, ~lo):
                s = jnp.where(mask, _dot_nt(jnp.where(hm, qb, jnp.zeros_like(qb)), kb), NEG_INF)
                stats.append((s, jnp.max(s, axis=1, keepdims=True)))
        probs = []
        ones = jnp.ones((2 * BLK, LANES), BF16)
        for s, m in stats:
            p = jnp.exp(s - m).astype(BF16)
            probs.append((p, _dot(p, ones)))
        for bi, (qs, qb, kb, vb, mask) in enumerate(blocks):
            (p0, l0), (p1, l1) = probs[2 * bi], probs[2 * bi + 1]
            m0, m1 = stats[2 * bi][1], stats[2 * bi + 1][1]
            zero = jnp.zeros_like(vb)
            acc = _dot(p0, jnp.where(lo_k, vb, zero)) + _dot(p1, jnp.where(lo_k, zero, vb))
            rows = pl.ds(qs, BLK, stride=stride)
            l_pair = jnp.where(lo, l0, l1)
            acc_s[ci, rows, :] = acc / l_pair
            lse_s[ci, rows, :] = jnp.where(lo, m0, m1) + jnp.log(l_pair)


def _attn_merge(o_ref, acc_s, lse_s):
    n_cfg = len(DILATED_CONFIGS)
    rc = 256
    for c0 in range(0, SUPER, rc):
        rows = slice(c0, c0 + rc)
        ls = [lse_s[ci, rows, :] for ci in range(n_cfg)]
        mx = functools.reduce(jnp.maximum, ls)
        num = jnp.zeros((rc, LANES), F32)
        den = jnp.zeros((rc, LANES), F32)
        for ci in range(n_cfg):
            a = jnp.exp(ls[ci] - mx)
            num = num + a * acc_s[ci, rows, :]
            den = den + a
        o_ref[0, rows, :] = (num / den).astype(o_ref.dtype)


def _tail_kernel(x_ref, mod_ref, attn_ref, yssm_ref, gattn_ref, gfin_ref, wout_ref, wup_ref, wdown_ref, y_ref,
                 *, sample, tm):
    if sample:
        mod = lambda k: mod_ref[0:tm, k * D_MODEL:(k + 1) * D_MODEL]
        attn = attn_ref[...].T
    else:
        mod = lambda k: mod_ref[0, k:k + 1, :]
        attn = attn_ref[0].astype(F32)
    x = x_ref[...].reshape(tm, D_MODEL)
    attn_n = (_rms(attn) * gattn_ref[...]).astype(BF16)
    mix = _dot(attn_n, wout_ref[0:D_ATTN, :]) + _dot(yssm_ref[...].reshape(tm, D_SSM), wout_ref[D_ATTN:, :])
    x1 = x + mod(2) * mix
    h2 = (_rms(x1) * (1.0 + mod(4)) + mod(3)).astype(BF16)
    d_ff = wup_ref.shape[1]
    cw = 1024
    acc = jnp.zeros((tm, D_MODEL), F32)
    for cc in range(d_ff // cw):
        u = jnp.maximum(_dot(h2, wup_ref[:, cc * cw:(cc + 1) * cw]), 0.0)
        acc = acc + _dot((u * u).astype(BF16), wdown_ref[cc * cw:(cc + 1) * cw, :])
    x2 = x1 + mod(5) * acc
    y_ref[...] = (_rms(x2) * gfin_ref[...]).reshape(y_ref.shape)


def _tail_prompt(x, mod, attn, y_ssm, g_attn, g_final, w_out, w_up, w_down, tm=512):
    b, s, d = x.shape
    row = lambda bi, i: (bi, i, 0)
    in_specs = [pl.BlockSpec((1, tm, d), row), pl.BlockSpec((1, 6, d), lambda bi, i: (bi, 0, 0)),
                pl.BlockSpec((1, tm, D_ATTN), row), pl.BlockSpec((1, tm, D_SSM), row),
                _const_spec(g_attn.shape), _const_spec(g_final.shape),
                _const_spec(w_out.shape), _const_spec(w_up.shape), _const_spec(w_down.shape)]
    return pl.pallas_call(
        functools.partial(_tail_kernel, sample=False, tm=tm),
        grid=(b, s // tm), in_specs=in_specs,
        out_specs=pl.BlockSpec((1, tm, d), row),
        out_shape=jax.ShapeDtypeStruct((b, s, d), F32),
        compiler_params=pltpu.CompilerParams(dimension_semantics=("arbitrary", "arbitrary"),
                                             vmem_limit_bytes=VMEM_LIMIT),
        name="tail_prompt",
    )(x, mod, attn, y_ssm, g_attn, g_final, w_out, w_up, w_down)


def _tail_sample(x, mod_t, attn, y_ssm, g_attn, g_final, w_out, w_up, w_down):
    m, d = x.shape
    full = lambda shape: pl.BlockSpec(shape, lambda i: (0,) * len(shape))
    in_specs = [full((m, d)), full(mod_t.shape), full((D_ATTN, m)), full((m, D_SSM)),
                full(g_attn.shape), full(g_final.shape), full(w_out.shape), full(w_up.shape), full(w_down.shape)]
    return pl.pallas_call(
        functools.partial(_tail_kernel, sample=True, tm=m),
        grid=(1,), in_specs=in_specs,
        out_specs=full((m, d)),
        out_shape=jax.ShapeDtypeStruct((m, d), F32),
        compiler_params=pltpu.CompilerParams(dimension_semantics=("arbitrary",), vmem_limit_bytes=VMEM_LIMIT),
        name="tail_sample",
    )(x, mod_t, attn, y_ssm, g_attn, g_final, w_out, w_up, w_down)


def _inproj_sample_kernel(x_ref, mod_ref, w_ref, wdt_ref, cos_ref, sin_ref, convst_ref, convw_ref, convb_ref,
                          dtb_ref, qT_ref, kT_ref, vT_ref, z_ref, xbc_ref, xT_ref, dtT_ref, convout_ref):
    m = x_ref.shape[0]
    h = _rms(x_ref[...]) * (1.0 + mod_ref[0:m, D_MODEL:2 * D_MODEL]) + mod_ref[0:m, 0:D_MODEL]
    hb = h.astype(BF16)
    cos = cos_ref[...]
    sin = sin_ref[...]
    lane = lax.broadcasted_iota(jnp.int32, (m, LANES), 1)
    lower = (lane % HEAD_DIM) < (HEAD_DIM // 2)
    for c in range(D_ATTN // LANES):
        sl = slice(c * LANES, (c + 1) * LANES)
        qc = _rope_chunk(_dot(hb, w_ref[:, c * LANES:(c + 1) * LANES]), cos, sin, lower)
        qT_ref[sl, :] = (qc * (HEAD_DIM ** -0.5)).T
        kc = _rope_chunk(_dot(hb, w_ref[:, D_ATTN + c * LANES:D_ATTN + (c + 1) * LANES]), cos, sin, lower)
        kT_ref[sl, :] = kc.T
        vT_ref[sl, :] = _dot(hb, w_ref[:, 2 * D_ATTN + c * LANES:2 * D_ATTN + (c + 1) * LANES]).T
    z_ref[...] = _dot(hb, w_ref[:, 3 * D_ATTN:3 * D_ATTN + D_SSM])
    base = 3 * D_ATTN + D_SSM
    xbc = _dot(hb, w_ref[:, base:base + D_CONV])
    acc = convb_ref[...] + convw_ref[CONV_WIDTH - 1:CONV_WIDTH, :] * xbc
    for w in range(CONV_WIDTH - 1):
        acc = acc + convw_ref[w:w + 1, :] * convst_ref[w]
    xc = _silu(acc)
    xbc_ref[...] = xc
    xT_ref[...] = xc[:, :D_SSM].T
    for w in range(CONV_WIDTH - 2):
        convout_ref[w] = convst_ref[w + 1]
    convout_ref[CONV_WIDTH - 2] = xbc
    dt = _softplus(_dot(hb, wdt_ref[...]) + dtb_ref[...])
    dtT_ref[...] = dt.T[:N_HEADS, :]


def _inproj_sample(x, mod_t, w_main, w_dt, cos, sin, conv_state, conv_w, conv_b, dtb_pad):
    m, d = x.shape
    full = lambda shape: pl.BlockSpec(shape, lambda i: (0,) * len(shape))
    ins = (x, mod_t, w_main, w_dt, cos, sin, conv_state, conv_w, conv_b, dtb_pad)
    out_shape = [
        jax.ShapeDtypeStruct((D_ATTN, m), F32), jax.ShapeDtypeStruct((D_ATTN, m), F32),
        jax.ShapeDtypeStruct((D_ATTN, m), F32), jax.ShapeDtypeStruct((m, D_SSM), F32),
        jax.ShapeDtypeStruct((m, D_CONV), F32), jax.ShapeDtypeStruct((D_SSM, m), F32),
        jax.ShapeDtypeStruct((N_HEADS, m), F32), jax.ShapeDtypeStruct((CONV_WIDTH - 1, m, D_CONV), F32),
    ]
    return pl.pallas_call(
        _inproj_sample_kernel, grid=(1,),
        in_specs=[full(a.shape) for a in ins],
        out_specs=[full(o.shape) for o in out_shape],
        out_shape=out_shape,
        compiler_params=pltpu.CompilerParams(dimension_semantics=("arbitrary",), vmem_limit_bytes=VMEM_LIMIT),
        name="inproj_sample",
    )(*ins)


def _shift_heads(b, heads, qT_ref, knT_ref, vnT_ref, kc_ref, vc_ref, oT_ref, ko_ref, vo_ref, *, win, nb):
    sel = lax.broadcasted_iota(jnp.int32, (HEAD_DIM, nb), 1) == b
    pick = lambda ref, rs: jnp.sum(jnp.where(sel, ref[rs, :], 0.0), axis=1, keepdims=True)
    nt = win // LANES
    lane = lax.broadcasted_iota(jnp.int32, (HEAD_DIM, LANES), 1)
    keep = lane < LANES - 1
    lane1 = lax.broadcasted_iota(jnp.int32, (1, LANES), 1)
    n_cfg = float(len(DILATED_CONFIGS))

    def mult(c):
        w = lane1 + c * LANES
        tot = jnp.zeros((1, LANES), F32)
        for wd, dil in DILATED_CONFIGS:
            tot = tot + ((w >= win - wd) & ((win - w) % dil == 0)).astype(F32)
        return tot

    mults = [mult(c) for c in range(nt)]
    mults_cat = jnp.concatenate(mults, axis=0)

    for h in heads:
        rs = slice(h * HEAD_DIM, (h + 1) * HEAD_DIM)
        qc = pick(qT_ref, rs)
        knc = pick(knT_ref, rs)
        vnc = pick(vnT_ref, rs)
        s_self = jnp.sum(qc * knc, axis=0, keepdims=True)
        s_tiles = [None] * nt
        nxt = jnp.broadcast_to(knc, (HEAD_DIM, LANES))
        for c in reversed(range(nt)):
            tl = slice(c * LANES, (c + 1) * LANES)
            kt = kc_ref[0, h, :, tl]
            s_tiles[c] = jnp.where(mults[c] > 0, jnp.sum(kt * qc, axis=0, keepdims=True), NEG_INF)
            r = pltpu.roll(kt, LANES - 1, 1)
            ko_ref[0, h, :, tl] = jnp.where(keep, r, nxt)
            nxt = r
        m = jnp.maximum(jnp.max(functools.reduce(jnp.maximum, s_tiles), axis=1, keepdims=True), s_self)
        p_self = n_cfg * jnp.exp(s_self - m)
        p_cat = mults_cat * jnp.exp(jnp.concatenate(s_tiles, axis=0) - m)
        p_tiles = [p_cat[c:c + 1, :] for c in range(nt)]
        l = jnp.sum(jnp.sum(p_cat, axis=0, keepdims=True), axis=1, keepdims=True) + p_self
        acc = jnp.zeros((HEAD_DIM, LANES), F32)
        nxt = jnp.broadcast_to(vnc, (HEAD_DIM, LANES))
        for c in reversed(range(nt)):
            tl = slice(c * LANES, (c + 1) * LANES)
            vt = vc_ref[0, h, :, tl]
            acc = acc + vt * p_tiles[c]
            r = pltpu.roll(vt, LANES - 1, 1)
            vo_ref[0, h, :, tl] = jnp.where(keep, r, nxt)
            nxt = r
        o = jnp.sum(acc, axis=1, keepdims=True) + p_self * vnc
        oT_ref[rs, :] = jnp.where(sel, o / l, oT_ref[rs, :])


def _attn_shift_kernel(qT_ref, knT_ref, vnT_ref, kc_ref, vc_ref, q_ref, kp_ref, kcur_ref, vp_ref, vcur_ref,
                       oT_ref, ko_ref, vo_ref, o_ref, qf, kf, vf, acc_s, lse_s, *, hpb, win, nb, n_tiles):
    b = pl.program_id(1)
    step = pl.program_id(0) * nb + b
    phase = step % N_PHASE
    t = (step // N_PHASE) % n_tiles

    @pl.when(b == 0)
    def _():
        oT_ref[...] = jnp.zeros_like(oT_ref)

    @pl.when(phase == 0)
    def _():
        _attn_stage(t, q_ref, kp_ref, kcur_ref, vp_ref, vcur_ref, qf, kf, vf)

    n_cfg = len(DILATED_CONFIGS)
    for ci, (_, dil) in enumerate(DILATED_CONFIGS):
        _attn_group(ci, dil, phase, t, qf, kf, vf, acc_s, lse_s)
        heads = range(ci * hpb // n_cfg, (ci + 1) * hpb // n_cfg)
        _shift_heads(b, heads, qT_ref, knT_ref, vnT_ref, kc_ref, vc_ref, oT_ref, ko_ref, vo_ref, win=win, nb=nb)

    @pl.when(phase == N_PHASE - 1)
    def _():
        _attn_merge(o_ref, acc_s, lse_s)


def _attn_shift(qT, knT, vnT, cache_kT, cache_vT, q, k, v, hpb=8):
    nb, nh, hd, win = cache_kT.shape
    bp, s, d = q.shape
    n_tiles, n_pairs = s // SUPER, d // LANES
    n_steps = (nh // hpb) * nb
    assert bp * n_pairs * n_tiles * N_PHASE == n_steps, "attention tiles must fill the cache-shift grid exactly"
    n_cfg = len(DILATED_CONFIGS)
    colspec = pl.BlockSpec((hpb * hd, nb), lambda g, b: (g, 0))
    cspec = pl.BlockSpec((1, hpb, hd, win), lambda g, b: (b, g, 0, 0))

    def tile_index(g, b, prev):
        a = (g * nb + b) // N_PHASE
        t = a % n_tiles
        return (a // (n_pairs * n_tiles), jnp.maximum(t - 1, 0) if prev else t, (a // n_tiles) % n_pairs)

    cur = lambda g, b: tile_index(g, b, False)
    prev = lambda g, b: tile_index(g, b, True)
    blk = (1, SUPER, LANES)
    return pl.pallas_call(
        functools.partial(_attn_shift_kernel, hpb=hpb, win=win, nb=nb, n_tiles=n_tiles),
        grid=(nh // hpb, nb),
        in_specs=[colspec, colspec, colspec, cspec, cspec,
                  pl.BlockSpec(blk, cur), pl.BlockSpec(blk, prev), pl.BlockSpec(blk, cur),
                  pl.BlockSpec(blk, prev), pl.BlockSpec(blk, cur)],
        out_specs=[colspec, cspec, cspec, pl.BlockSpec(blk, cur)],
        out_shape=[jax.ShapeDtypeStruct((nh * hd, nb), F32),
                   jax.ShapeDtypeStruct(cache_kT.shape, F32), jax.ShapeDtypeStruct(cache_vT.shape, F32),
                   jax.ShapeDtypeStruct((bp, s, d), BF16)],
        scratch_shapes=[pltpu.VMEM((SUPER, LANES), F32), pltpu.VMEM((2 * SUPER, LANES), F32),
                        pltpu.VMEM((2 * SUPER, LANES), F32), pltpu.VMEM((n_cfg, SUPER, LANES), F32),
                        pltpu.VMEM((n_cfg, SUPER, LANES), F32)],
        compiler_params=pltpu.CompilerParams(dimension_semantics=("arbitrary", "arbitrary"),
                                             vmem_limit_bytes=VMEM_LIMIT),
        name="attn_shift",
    )(qT, knT, vnT, cache_kT, cache_vT, q, k, k, v, v)


def _ssm_sample_kernel(st_ref, xT_ref, b_ref, c_ref, dtT_ref, alogT_ref, new_ref, yT_ref, *, nb):
    h = pl.program_id(0)
    a_col = -jnp.exp(alogT_ref[...])
    dtT = dtT_ref[...]
    decT = jnp.exp(dtT * a_col)
    hsel = lax.broadcasted_iota(jnp.int32, (N_HEADS, nb), 0) == h
    dt_h = jnp.sum(jnp.where(hsel, dtT, 0.0), axis=0, keepdims=True)
    dec_h = jnp.sum(jnp.where(hsel, decT, 0.0), axis=0, keepdims=True)
    xdt = xT_ref[...] * dt_h
    lane = lax.broadcasted_iota(jnp.int32, (HEAD_DIM, nb), 1)
    y_acc = jnp.zeros((HEAD_DIM, nb), F32)
    rows_per_batch = 8
    for b0 in range(0, nb, rows_per_batch):
        prods = []
        for bi in range(b0, b0 + rows_per_batch):
            outer = xdt[:, bi:bi + 1] * b_ref[bi:bi + 1, :]
            new = st_ref[0, bi, 0] * dec_h[:, bi:bi + 1] + outer
            new_ref[0, bi, 0] = new
            prods.append(new * c_ref[bi:bi + 1, :])
        cols = [jnp.sum(p, axis=1, keepdims=True) for p in prods]
        for k, ycol in enumerate(cols):
            y_acc = jnp.where(lane == b0 + k, ycol, y_acc)
    yT_ref[...] = y_acc


def _ssm_sample(state, xT, xbc, dtT, a_logT):
    _, nb, nh, hp, n = state.shape
    hg = nh // SSM_GROUPS
    b_blk0 = D_SSM // D_STATE
    return pl.pallas_call(
        functools.partial(_ssm_sample_kernel, nb=nb),
        grid=(nh,),
        in_specs=[pl.BlockSpec((1, nb, 1, hp, n), lambda h: (0, 0, h, 0, 0)),
                  pl.BlockSpec((hp, nb), lambda h: (h, 0)),
                  pl.BlockSpec((nb, D_STATE), lambda h: (0, b_blk0 + h // hg)),
                  pl.BlockSpec((nb, D_STATE), lambda h: (0, b_blk0 + SSM_GROUPS + h // hg)),
                  pl.BlockSpec((nh, nb), lambda h: (0, 0)),
                  pl.BlockSpec((nh, 1), lambda h: (0, 0))],
        out_specs=[pl.BlockSpec((1, nb, 1, hp, n), lambda h: (0, 0, h, 0, 0)),
                   pl.BlockSpec((hp, nb), lambda h: (h, 0))],
        out_shape=[jax.ShapeDtypeStruct(state.shape, F32), jax.ShapeDtypeStruct((nh * hp, nb), F32)],
        compiler_params=pltpu.CompilerParams(dimension_semantics=("arbitrary",), vmem_limit_bytes=VMEM_LIMIT),
        name="ssm_sample",
    )(state, xT, xbc, xbc, dtT, a_logT)


def _ssm_gate_kernel(yT_ref, xs_ref, z_ref, dskip_ref, gssm_ref, o_ref):
    y = yT_ref[...].T + dskip_ref[...] * xs_ref[...]
    y = y * _silu(z_ref[...])
    gw = D_SSM // SSM_GROUPS
    for g in range(SSM_GROUPS):
        sl = slice(g * gw, (g + 1) * gw)
        o_ref[:, sl] = (_rms(y[:, sl]) * gssm_ref[:, sl]).astype(BF16)


def _ssm_gate(yT, xbc, z, dskip_lane, g_ssm):
    m = z.shape[0]
    full = lambda shape: pl.BlockSpec(shape, lambda i: (0,) * len(shape))
    return pl.pallas_call(
        _ssm_gate_kernel, grid=(1,),
        in_specs=[full(yT.shape), pl.BlockSpec((m, D_SSM), lambda i: (0, 0)), full(z.shape),
                  full(dskip_lane.shape), full(g_ssm.shape)],
        out_specs=full((m, D_SSM)),
        out_shape=jax.ShapeDtypeStruct((m, D_SSM), BF16),
        compiler_params=pltpu.CompilerParams(dimension_semantics=("arbitrary",), vmem_limit_bytes=VMEM_LIMIT),
        name="ssm_gate_sample",
    )(yT, xbc, z, dskip_lane, g_ssm)


def _rope_tables(pos):
    half = HEAD_DIM // 2
    inv_freq = ROPE_THETA ** (-jnp.arange(half, dtype=F32) / half)
    ang = pos.astype(F32)[:, None] * inv_freq[None, :]
    cos = jnp.tile(jnp.cos(ang), (1, LANES // half))
    sin = jnp.tile(jnp.sin(ang), (1, LANES // half))
    sign = jnp.where((jnp.arange(LANES) % HEAD_DIM) < half, -1.0, 1.0).astype(F32)
    return cos, sin * sign[None, :]


def kernel(x_prompt, x_sample, c_prompt, c_sample, cache_k_win, cache_v_win, state_conv, state_ssm, w_ada, b_ada,
           w_in, conv_w, conv_b, dt_bias, a_log, d_skip, g_attn, g_ssm, w_out, w_up, w_down, g_final):
    depth = w_ada.shape[0]
    assert depth == 1, "single-layer trunk"
    b, s, d = x_prompt.shape
    nb, t_new, _ = x_sample.shape
    assert t_new == 1
    win = cache_k_win.shape[2]
    assert win == MAX_WINDOW and s % MAX_WINDOW == 0
    past_len = PAST_LEN
    layer = 0

    w_in_l = w_in[layer]
    w_main = w_in_l.astype(BF16)
    w_dt = jnp.pad(w_in_l[:, D_MAIN:], ((0, 0), (0, LANES - N_HEADS))).astype(BF16)
    dtb_pad = jnp.pad(dt_bias[layer].astype(F32), (0, LANES - N_HEADS))[None, :]
    w_out_b = w_out[layer].astype(BF16)
    w_up_b = w_up[layer].astype(BF16)
    w_down_b = w_down[layer].astype(BF16)
    conv_w_l = conv_w[layer]
    conv_b_l = conv_b[layer][None, :]
    a_log_row = a_log[layer][None, :]
    a_log_col = a_log[layer][:, None]
    dskip_lane = jnp.repeat(d_skip[layer].astype(F32), HEAD_DIM)[None, :]
    g_attn_l = g_attn[layer][None, :]
    g_ssm_l = g_ssm[layer][None, :]
    g_final_l = g_final[None, :]

    pad_rows = (-(nb + b)) % 8
    c_all = jnp.concatenate([c_sample, c_prompt, jnp.zeros((pad_rows, d), F32)], axis=0)
    mod = _ada(c_all, w_ada[layer], b_ada[layer][None, :])
    mod_s = mod
    mod_p = mod[nb:nb + b].reshape(b, 6, d)

    cos_p, sin_p = _rope_tables(jnp.arange(s, dtype=jnp.int32))
    q, k, v, z, xbc, dt, dtT, k_win, v_win, conv_p = _inproj_prompt(
        x_prompt, mod_p, w_main, w_dt, cos_p, sin_p, conv_w_l, conv_b_l, dtb_pad)
    y_ssm, h_last = _ssd_prompt(xbc, z, dt, dtT, a_log_row, a_log_col, dskip_lane, g_ssm_l)

    cos_s, sin_s = _rope_tables(past_len + jnp.arange(1, dtype=jnp.int32))
    qT_s, kT_s, vT_s, z_s, xbc_s, xT_s, dtT_s, conv_s = _inproj_sample(
        x_sample.reshape(nb, d), mod_s, w_main, w_dt, cos_s, sin_s, state_conv[layer].transpose(1, 0, 2),
        conv_w_l, conv_b_l, dtb_pad)

    attnT_s, k_winT_s, v_winT_s, attn = _attn_shift(qT_s, kT_s, vT_s,
                                                    cache_k_win[layer].transpose(0, 2, 3, 1),
                                                    cache_v_win[layer].transpose(0, 2, 3, 1), q, k, v)
    y_prompt = _tail_prompt(x_prompt, mod_p, attn, y_ssm, g_attn_l, g_final_l, w_out_b, w_up_b, w_down_b)
    ssm_s, yT_s = _ssm_sample(state_ssm[layer][None], xT_s, xbc_s, dtT_s, a_log_col)
    y_ssm_s = _ssm_gate(yT_s, xbc_s, z_s, dskip_lane, g_ssm_l)
    y_sample = _tail_sample(x_sample.reshape(nb, d), mod_s, attnT_s, y_ssm_s,
                            g_attn_l, g_final_l, w_out_b, w_up_b, w_down_b)

    hd = (N_HEADS, HEAD_DIM)
    return (y_prompt, y_sample.reshape(nb, 1, d),
            k_win.reshape(1, b, -1, *hd), v_win.reshape(1, b, -1, *hd),
            conv_p[None], h_last.reshape(1, b, N_HEADS, HEAD_DIM, D_STATE),
            k_winT_s.transpose(0, 3, 1, 2)[None], v_winT_s.transpose(0, 3, 1, 2)[None],
            conv_s.transpose(1, 0, 2)[None], ssm_s)
```
